```python
import math
import jax, jax.numpy as jnp
from jax import lax
import numpy as np

D_MODEL = 1024
BATCH = 8
SEQ = 2048
DEPTH = 4

GRID_W = 64
CTX_LEN = 256
Q_BLOCK = 128

LRU_WIDTH = 256
LRU_BLOCKS = 4
LRU_C = 8.0
LRU_CONV = 4
LRU_CONV_LEFT = 2
DIFF_HEADS = 4
DIFF_D = 32
DIFF_V = 2 * DIFF_D
GQA_Q_HEADS = 4
GQA_KV_HEADS = 2
GQA_D = 64
NA_HEADS = 4
NA_D = 64
NA_WIN_ROWS = 8
NA_WIN_COLS = 16

N_BRANCH = 4
BRANCH_W = 256
D_FF = 2816
FFN_CONV = 3
ROPE_BASE = 10000.0
ALPHA = (2 * DEPTH) ** 0.25
BETA = (8 * DEPTH) ** -0.25
EPS = 1e-6

IN_WIDTHS = (LRU_WIDTH, LRU_WIDTH,
             DIFF_HEADS * 2 * DIFF_D, DIFF_HEADS * 2 * DIFF_D, DIFF_HEADS * DIFF_V,
             GQA_Q_HEADS * GQA_D, GQA_KV_HEADS * GQA_D, GQA_KV_HEADS * GQA_D,
             NA_HEADS * NA_D, NA_HEADS * NA_D, NA_HEADS * NA_D,
             N_BRANCH * D_MODEL)
D_IN = sum(IN_WIDTHS)

kernel_name = 'hybrid_diffusion_parallel_mixer_trunk'

F32 = jnp.float32


def layer_norm(x, g=None, b=None):
    xf = x.astype(F32)
    mu = jnp.mean(xf, -1, keepdims=True)
    var = jnp.mean(jnp.square(xf - mu), -1, keepdims=True)
    y = (xf - mu) * lax.rsqrt(var + EPS)
    if g is not None:
        y = y * g + b
    return y.astype(x.dtype)


def rms_norm(x, g):
    xf = x.astype(F32)
    return (xf * lax.rsqrt(jnp.mean(xf * xf, -1, keepdims=True) + EPS) * g).astype(x.dtype)


def modulate(x, shift, scale):
    return layer_norm(x) * (1 + scale) + shift


def dwconv(x, w, b, left):
    K, C = w.shape
    y = lax.conv_general_dilated(x, w[:, None, :], window_strides=(1,),
                                 padding=[(left, K - 1 - left)],
                                 dimension_numbers=('NWC', 'WIO', 'NWC'),
                                 feature_group_count=C)
    return y + b


def axial_rope(n_tok, dim):
    t = jnp.arange(n_tok)
    row = (t // GRID_W).astype(F32)
    col = (t % GRID_W).astype(F32)
    n_freq = dim // 4
    inv = ROPE_BASE ** (-jnp.arange(n_freq, dtype=F32) / n_freq)
    ang = jnp.concatenate([row[:, None] * inv, col[:, None] * inv], -1)
    return jnp.cos(ang), jnp.sin(ang)


def apply_rope(x, cos, sin):
    half = x.shape[-1] // 2
    xf = x.astype(F32)
    x1, x2 = xf[..., :half], xf[..., half:]
    c, s = cos[:, None], sin[:, None]
    return jnp.concatenate([x1 * c - x2 * s, x1 * s + x2 * c], -1).astype(x.dtype)


def split_columns(z):
    idx, s = [], 0
    for w in IN_WIDTHS[:-1]:
        s += w
        idx.append(s)
    return jnp.split(z, idx, axis=-1)


def to_blocks(t):
    B, S = t.shape[:2]
    return jnp.swapaxes(t.reshape(B, S // Q_BLOCK, Q_BLOCK, *t.shape[2:]), 0, 1)


def from_blocks(t):
    nb, B, qb = t.shape[:3]
    return jnp.swapaxes(t, 0, 1).reshape(B, nb * qb, *t.shape[3:])


def sweep_queries(fn, q):
    return from_blocks(lax.map(fn, to_blocks(q)))


def rglru_coeffs(xc, w_a, b_a, w_x, b_x, lam):
    B, T, W = xc.shape
    xb = xc.reshape(B, T, LRU_BLOCKS, W // LRU_BLOCKS)
    r = jax.nn.sigmoid((jnp.einsum('btnc,ncd->btnd', xb, w_a) + b_a).astype(F32)).reshape(B, T, W)
    i = jax.nn.sigmoid((jnp.einsum('btnc,ncd->btnd', xb, w_x) + b_x).astype(F32)).reshape(B, T, W)
    log_a = -LRU_C * r * jax.nn.softplus(-lam.astype(F32))
    a = jnp.exp(log_a)
    mult = jnp.sqrt(-jnp.expm1(2.0 * log_a))
    return a, mult * i * xc.astype(F32)


def _combine(e1, e2):
    a1, b1 = e1
    a2, b2 = e2
    return a1 * a2, a2 * b1 + b2


def linear_scan(a, b, h0, reverse):
    a_cum, b_cum = lax.associative_scan(_combine, (a, b), axis=1, reverse=reverse)
    return b_cum + a_cum * h0[:, None]


def rglru_branch(x_lat, g_lat, x_ctx, g_ctx, with_ctx, conv_w, conv_b, w_a, b_a, w_x, b_x, lam):
    B = x_lat.shape[0]
    xl = dwconv(x_lat, conv_w, conv_b, LRU_CONV_LEFT)
    xc = dwconv(x_ctx, conv_w, conv_b, LRU_CONV_LEFT)
    h_lat, h_ctx = [], []
    for d, reverse in enumerate((False, True)):
        p = (w_a[d], b_a[d], w_x[d], b_x[d], lam[d])
        hc = linear_scan(*rglru_coeffs(xc, *p), jnp.zeros((B, LRU_WIDTH), F32), reverse)
        h_end = hc[:, 0] if reverse else hc[:, -1]
        h_lat.append(linear_scan(*rglru_coeffs(xl, *p), h_end, reverse))
        h_ctx.append(hc)
    y_lat = (h_lat[0] + h_lat[1]).astype(x_lat.dtype) * jax.nn.gelu(g_lat)
    y_ctx = (h_ctx[0] + h_ctx[1]).astype(x_ctx.dtype) * jax.nn.gelu(g_ctx) if with_ctx else None
    return y_lat, y_ctx


def diff_attend(q, k, v, lam):
    s = jnp.einsum('bqhmd,bkhmd->bhmqk', q, k).astype(F32) * DIFF_D ** -0.5
    p = jax.nn.softmax(s, axis=-1)
    w = p[:, :, 0] - lam * p[:, :, 1]
    return jnp.einsum('bhqk,bkhd->bqhd', w.astype(v.dtype), v)


def gqa_attend(q, k, v):
    s = jnp.einsum('bqhgd,bkhd->bhgqk', q, k).astype(F32) * q.shape[-1] ** -0.5
    p = jax.nn.softmax(s, axis=-1).astype(v.dtype)
    return jnp.einsum('bhgqk,bkhd->bqhgd', p, v)


def neighbourhood_attention(q, k, v, k_ctx, v_ctx, rpb):
    B, S, H, d = q.shape
    rows = S // GRID_W
    wr = min(NA_WIN_ROWS, rows)
    wc = NA_WIN_COLS
    scale = d ** -0.5
    qg = q.reshape(B, rows, GRID_W, H, d)
    kg = k.reshape(B, rows, GRID_W, H, d)
    vg = v.reshape(B, rows, GRID_W, H, d)
    row_start = jnp.clip(jnp.arange(rows) - wr // 2, 0, rows - wr)
    col_q = jnp.arange(GRID_W)
    col_idx = jnp.clip(col_q - wc // 2, 0, GRID_W - wc)[:, None] + jnp.arange(wc)
    bias_cols = rpb[:, :, col_idx - col_q[:, None] + (NA_WIN_COLS - 1)]

    def one_row(args):
        q_row, r = args
        r0 = row_start[r]
        k_win = lax.dynamic_slice_in_dim(kg, r0, wr, axis=1)[:, :, col_idx]
        v_win = lax.dynamic_slice_in_dim(vg, r0, wr, axis=1)[:, :, col_idx]
        row_off = r0 + jnp.arange(wr) - r + (NA_WIN_ROWS - 1)
        bias = jnp.transpose(bias_cols[:, row_off], (0, 2, 1, 3)).astype(F32)
        s_win = jnp.einsum('bqhd,brqwhd->bhqrw', q_row, k_win).astype(F32) * scale + bias
        s_ctx = jnp.einsum('bqhd,bchd->bhqc', q_row, k_ctx).astype(F32) * scale
        s = jnp.concatenate([s_win.reshape(B, H, GRID_W, wr * wc), s_ctx], -1)
        p = jax.nn.softmax(s, axis=-1).astype(v.dtype)
        p_win = p[..., :wr * wc].reshape(B, H, GRID_W, wr, wc)
        p_ctx = p[..., wr * wc:]
        return (jnp.einsum('bhqrw,brqwhd->bqhd', p_win, v_win)
                + jnp.einsum('bhqc,bchd->bqhd', p_ctx, v_ctx))

    out = lax.map(one_row, (jnp.swapaxes(qg, 0, 1), jnp.arange(rows)))
    return jnp.swapaxes(out, 0, 1).reshape(B, S, H, d)


def gated_merge(ys, gate_cols, w_branch, w_out, b_out):
    m = 0
    for j, y in enumerate(ys):
        g = jax.nn.sigmoid(gate_cols[..., j * D_MODEL:(j + 1) * D_MODEL])
        m = m + g * (y @ w_branch[j])
    return m @ w_out + b_out


def mixer_block(h, hc, with_ctx, lam_init, w_in, b_in, lru_conv_w, lru_conv_b, lru_w_a, lru_b_a,
                lru_w_x, lru_b_x, lru_lambda, diff_lambda, diff_subln, gqa_q_norm, gqa_k_norm,
                na_rpb, w_branch, w_out, b_out):
    B, S, _ = h.shape
    C = hc.shape[1]
    (a_x, a_g, d_q, d_k, d_v, g_q, g_k, g_v, n_q, n_k, n_v, gates) = split_columns(h @ w_in + b_in)
    (a_xc, a_gc, d_qc, d_kc, d_vc, g_qc, g_kc, g_vc, n_qc, n_kc, n_vc, gates_c) = split_columns(hc @ w_in + b_in)

    y_a, y_ac = rglru_branch(a_x, a_g, a_xc, a_gc, with_ctx, lru_conv_w, lru_conv_b,
                             lru_w_a, lru_b_a, lru_w_x, lru_b_x, lru_lambda)

    cos_d, sin_d = axial_rope(S, DIFF_D)
    lp = diff_lambda.astype(F32)
    lam = jnp.exp(jnp.sum(lp[0] * lp[1])) - jnp.exp(jnp.sum(lp[2] * lp[3])) + lam_init

    def diff_heads(t, T, rope):
        t = t.reshape(B, T, DIFF_HEADS * 2, DIFF_D)
        if rope:
            t = apply_rope(t, cos_d, sin_d)
        return t.reshape(B, T, DIFF_HEADS, 2, DIFF_D)

    dq, dk = diff_heads(d_q, S, True), diff_heads(d_k, S, True)
    dqc, dkc = diff_heads(d_qc, C, False), diff_heads(d_kc, C, False)
    dv = d_v.reshape(B, S, DIFF_HEADS, DIFF_V)
    dvc = d_vc.reshape(B, C, DIFF_HEADS, DIFF_V)
    dk_all = jnp.concatenate([dkc, dk], 1)
    dv_all = jnp.concatenate([dvc, dv], 1)
    o_b = sweep_queries(lambda qb: diff_attend(qb, dk_all, dv_all, lam), dq)
    y_b = (rms_norm(o_b, diff_subln) * (1.0 - lam_init)).reshape(B, S, -1)

    cos_g, sin_g = axial_rope(S, GQA_D)
    grp = GQA_Q_HEADS // GQA_KV_HEADS
    gq = apply_rope(rms_norm(g_q.reshape(B, S, GQA_Q_HEADS, GQA_D), gqa_q_norm), cos_g, sin_g)
    gk = apply_rope(rms_norm(g_k.reshape(B, S, GQA_KV_HEADS, GQA_D), gqa_k_norm), cos_g, sin_g)
    gqc = rms_norm(g_qc.reshape(B, C, GQA_Q_HEADS, GQA_D), gqa_q_norm)
    gkc = rms_norm(g_kc.reshape(B, C, GQA_KV_HEADS, GQA_D), gqa_k_norm)
    gv = g_v.reshape(B, S, GQA_KV_HEADS, GQA_D)
    gvc = g_vc.reshape(B, C, GQA_KV_HEADS, GQA_D)
    gk_all = jnp.concatenate([gkc, gk], 1)
    gv_all = jnp.concatenate([gvc, gv], 1)
    y_c = sweep_queries(lambda qb: gqa_attend(qb, gk_all, gv_all),
                        gq.reshape(B, S, GQA_KV_HEADS, grp, GQA_D)).reshape(B, S, -1)

    nqc = n_qc.reshape(B, C, NA_HEADS, NA_D)
    nkc = n_kc.reshape(B, C, NA_HEADS, NA_D)
    nvc = n_vc.reshape(B, C, NA_HEADS, NA_D)
    y_d = neighbourhood_attention(n_q.reshape(B, S, NA_HEADS, NA_D), n_k.reshape(B, S, NA_HEADS, NA_D),
                                  n_v.reshape(B, S, NA_HEADS, NA_D), nkc, nvc, na_rpb).reshape(B, S, -1)

    out = gated_merge((y_a, y_b, y_c, y_d), gates, w_branch, w_out, b_out)
    if not with_ctx:
        return out, None
    y_bc = (rms_norm(diff_attend(dqc, dkc, dvc, lam), diff_subln) * (1.0 - lam_init)).reshape(B, C, -1)
    y_cc = gqa_attend(gqc.reshape(B, C, GQA_KV_HEADS, grp, GQA_D), gkc, gvc).reshape(B, C, -1)
    y_dc = gqa_attend(nqc[:, :, :, None], nkc, nvc).reshape(B, C, -1)
    out_c = gated_merge((y_ac, y_bc, y_cc, y_dc), gates_c, w_branch, w_out, b_out)
    return out, out_c


def conv_ffn(h, w1, b1, conv_w, conv_b, w2, b2):
    u = dwconv(h @ w1 + b1, conv_w, conv_b, FFN_CONV // 2)
    val, gate = jnp.split(u, 2, axis=-1)
    return (jax.nn.silu(gate) * val) @ w2 + b2


def setup_inputs(seed: int = 0) -> dict:
    key = jax.random.key(seed)
    ks = iter(jax.random.split(key, 48))

    def nrm(shape, std):
        return std * jax.random.normal(next(ks), shape, F32)

    L, D = DEPTH, D_MODEL
    bw = LRU_WIDTH // LRU_BLOCKS
    u = jax.random.uniform(next(ks), (L, 2, LRU_WIDTH), F32, 0.9, 0.999)
    a_base = u ** (1.0 / LRU_C)
    lru_lambda = jnp.log(a_base) - jnp.log1p(-a_base)
    return {
        'x': nrm((BATCH, SEQ, D), 1.0),
        'c': nrm((BATCH, D), 1.0),
        'ctx': nrm((BATCH, CTX_LEN, D), 1.0),
        'c_ctx': nrm((D,), 1.0),
        'w_ada': nrm((L, D, 6 * D), 0.5 * D ** -0.5),
        'b_ada': nrm((L, 6 * D), 0.02),
        'w_in': nrm((L, D, D_IN), D ** -0.5),
        'b_in': nrm((L, D_IN), 0.02),
        'lru_conv_w': nrm((L, LRU_CONV, LRU_WIDTH), LRU_CONV ** -0.5),
        'lru_conv_b': nrm((L, LRU_WIDTH), 0.02),
        'lru_w_a': nrm((L, 2, LRU_BLOCKS, bw, bw), bw ** -0.5),
        'lru_b_a': nrm((L, 2, LRU_BLOCKS, bw), 0.02),
        'lru_w_x': nrm((L, 2, LRU_BLOCKS, bw, bw), bw ** -0.5),
        'lru_b_x': nrm((L, 2, LRU_BLOCKS, bw), 0.02),
        'lru_lambda': lru_lambda,
        'diff_lambda': nrm((L, 4, DIFF_D), 0.1),
        'diff_subln': 1.0 + nrm((L, DIFF_V), 0.02),
        'gqa_q_norm': 1.0 + nrm((L, GQA_D), 0.02),
        'gqa_k_norm': 1.0 + nrm((L, GQA_D), 0.02),
        'na_rpb': nrm((L, NA_HEADS, 2 * NA_WIN_ROWS - 1, 2 * NA_WIN_COLS - 1), 0.1),
        'w_branch': nrm((L, N_BRANCH, BRANCH_W, D), BRANCH_W ** -0.5),
        'w_out': nrm((L, D, D), BETA * D ** -0.5),
        'b_out': nrm((L, D), 0.02),
        'ln1_g': 1.0 + nrm((L, D), 0.02),
        'ln1_b': nrm((L, D), 0.02),
        'ffn_w1': nrm((L, D, 2 * D_FF), D ** -0.5),
        'ffn_b1': nrm((L, 2 * D_FF), 0.02),
        'ffn_conv_w': nrm((L, FFN_CONV, 2 * D_FF), FFN_CONV ** -0.5),
        'ffn_conv_b': nrm((L, 2 * D_FF), 0.02),
        'ffn_w2': nrm((L, D_FF, D), BETA * D_FF ** -0.5),
        'ffn_b2': nrm((L, D), 0.02),
        'ln2_g': 1.0 + nrm((L, D), 0.02),
        'ln2_b': nrm((L, D), 0.02),
    }


def reference(x, c, ctx, c_ctx, w_ada, b_ada, w_in, b_in, lru_conv_w, lru_conv_b, lru_w_a, lru_b_a,
              lru_w_x, lru_b_x, lru_lambda, diff_lambda, diff_subln, gqa_q_norm, gqa_k_norm, na_rpb,
              w_branch, w_out, b_out, ln1_g, ln1_b, ffn_w1, ffn_b1, ffn_conv_w, ffn_conv_b, ffn_w2,
              ffn_b2, ln2_g, ln2_b):
    xc = ctx
    silu_c = jax.nn.silu(c)
    silu_cc = jax.nn.silu(c_ctx)
    for l in range(DEPTH):
        with_ctx = l < DEPTH - 1
        lam_init = 0.8 - 0.6 * math.exp(-0.3 * l)
        mod = (silu_c @ w_ada[l] + b_ada[l])[:, None, :]
        mod_c = silu_cc @ w_ada[l] + b_ada[l]
        sh1, sc1, g1, sh2, sc2, g2 = jnp.split(mod, 6, axis=-1)
        sh1c, sc1c, g1c, sh2c, sc2c, g2c = jnp.split(mod_c, 6, axis=-1)

        m, mc = mixer_block(modulate(x, sh1, sc1), modulate(xc, sh1c, sc1c), with_ctx, lam_init,
                            w_in[l], b_in[l], lru_conv_w[l], lru_conv_b[l], lru_w_a[l], lru_b_a[l],
                            lru_w_x[l], lru_b_x[l], lru_lambda[l], diff_lambda[l], diff_subln[l],
                            gqa_q_norm[l], gqa_k_norm[l], na_rpb[l], w_branch[l], w_out[l], b_out[l])
        x = layer_norm(ALPHA * x + g1 * m, ln1_g[l], ln1_b[l])
        f = conv_ffn(modulate(x, sh2, sc2), ffn_w1[l], ffn_b1[l], ffn_conv_w[l], ffn_conv_b[l],
                     ffn_w2[l], ffn_b2[l])
        x = layer_norm(ALPHA * x + g2 * f, ln2_g[l], ln2_b[l])

        if with_ctx:
            xc = layer_norm(ALPHA * xc + g1c * mc, ln1_g[l], ln1_b[l])
            fc = conv_ffn(modulate(xc, sh2c, sc2c), ffn_w1[l], ffn_b1[l], ffn_conv_w[l], ffn_conv_b[l],
                          ffn_w2[l], ffn_b2[l])
            xc = layer_norm(ALPHA * xc + g2c * fc, ln2_g[l], ln2_b[l])
    return x
```

```python
import functools
import math

import numpy as np
import jax
import jax.numpy as jnp
from jax import lax
from jax.experimental import pallas as pl
from jax.experimental.pallas import tpu as pltpu

F32 = jnp.float32
MXU_DTYPE = jnp.bfloat16

DEPTH = 4
GRID_W = 64
LRU_WIDTH = 256
LRU_BLOCKS = 4
LRU_C = 8.0
LRU_CONV = 4
LRU_CONV_LEFT = 2
DIFF_HEADS = 4
DIFF_D = 32
DIFF_V = 64
GQA_Q_HEADS = 4
GQA_KV_HEADS = 2
GQA_D = 64
NA_HEADS = 4
NA_D = 64
NA_WIN_ROWS = 8
NA_WIN_COLS = 16
N_BRANCH = 4
BRANCH_W = 256
FFN_CONV = 3
ROPE_BASE = 10000.0
ALPHA = (2 * DEPTH) ** 0.25
EPS = 1e-6
NEG_BIG = -1e30

TILE = 256
SUBLANES = 8
NA_Q_ROWS = TILE // GRID_W
NA_K_ROWS = 12
FF_CHUNK = 768
SCAN_ROWS = 8
V7X_VMEM_LIMIT = 56 * 1024 * 1024

_IN_WIDTHS = (256, 256, 256, 256, 256, 256, 128, 128, 256, 256, 256)
_IN_OFFS = tuple(int(v) for v in np.cumsum((0,) + _IN_WIDTHS))


def _params(sem):
    return pltpu.CompilerParams(dimension_semantics=sem, vmem_limit_bytes=V7X_VMEM_LIMIT)


def _const_spec(shape):
    nd = len(shape)
    return pl.BlockSpec(shape, lambda *_: (0,) * nd)


def _ln(x):
    mu = jnp.mean(x, axis=-1, keepdims=True)
    xc = x - mu
    var = jnp.mean(xc * xc, axis=-1, keepdims=True)
    return xc * lax.rsqrt(var + EPS)


def _mm(a, b):
    return jnp.dot(a.astype(MXU_DTYPE), b, preferred_element_type=F32)


def _mm_nt(a, b):
    return lax.dot_general(a, b, (((1,), (1,)), ((), ())), preferred_element_type=F32)


def _seg_mean(x2, seg_ref):
    hi = x2.astype(MXU_DTYPE)
    lo = (x2 - hi.astype(F32)).astype(MXU_DTYPE)
    seg = seg_ref[...]
    return (jnp.dot(hi, seg, preferred_element_type=F32)
            + jnp.dot(lo, seg, preferred_element_type=F32))


def _lane_mask(width, lo, hi):
    lane = lax.broadcasted_iota(jnp.int32, (1, width), 1)
    return (lane >= lo) & (lane < hi)


def _rope(x, cos, sin_signed, dim):
    n = x.shape[-1]
    half = dim // 2
    lane = lax.broadcasted_iota(jnp.int32, (1, n), 1)
    first = (lane & (dim - 1)) < half
    swapped = jnp.where(first, pltpu.roll(x, n - half, 1), pltpu.roll(x, half, 1))
    return x * cos + swapped * sin_signed


def _ada_kernel(c_ref, w_ref, b_ref, o_ref):
    c = c_ref[...]
    s = c * jax.nn.sigmoid(c)
    o_ref[0] = _mm(s, w_ref[0].astype(MXU_DTYPE)) + b_ref[0]


def _ada(cvec, w_ada, b_ada):
    L, D, N = w_ada.shape
    tn = 1536
    R = cvec.shape[0]
    return pl.pallas_call(
        _ada_kernel,
        grid=(L, N // tn),
        in_specs=[pl.BlockSpec((R, D), lambda l, j: (0, 0)),
                  pl.BlockSpec((1, D, tn), lambda l, j: (l, 0, j)),
                  pl.BlockSpec((1, 1, tn), lambda l, j: (l, 0, j))],
        out_specs=pl.BlockSpec((1, R, tn), lambda l, j: (l, 0, j)),
        out_shape=jax.ShapeDtypeStruct((L, R, N), F32),
        compiler_params=_params(("parallel", "parallel")),
        name="ada_mod",
    )(cvec, w_ada, b_ada.reshape(L, 1, N))


def _in_kernel(x_ref, mod_ref, w_ref, b_ref, cd_ref, sd_ref, cg_ref, sg_ref, seg_ref, qn_ref, kn_ref,
               lru_ref, dq_ref, dk_ref, dv_ref, gq_ref, gk_ref, gv_ref, nq_ref, nk_ref, nv_ref,
               gates_ref):
    D = x_ref.shape[-1]
    mod = mod_ref[0]
    h = (_ln(x_ref[0]) * (1.0 + mod[:, D:2 * D]) + mod[:, 0:D]).astype(MXU_DTYPE)

    def proj(lo, hi):
        return jnp.dot(h, w_ref[:, lo:hi], preferred_element_type=F32) + b_ref[:, lo:hi]

    o = _IN_OFFS
    lru_ref[0] = proj(o[0], o[2])

    cd, sd = cd_ref[...], sd_ref[...]
    dq_ref[0] = (_rope(proj(o[2], o[3]), cd, sd, DIFF_D) * DIFF_D ** -0.5).astype(dq_ref.dtype)
    dk_ref[0] = _rope(proj(o[3], o[4]), cd, sd, DIFF_D).astype(dk_ref.dtype)
    dv_ref[0] = proj(o[4], o[5]).astype(dv_ref.dtype)

    cg, sg = cg_ref[...], sg_ref[...]
    gq = proj(o[5], o[6])
    gq = gq * lax.rsqrt(_seg_mean(gq * gq, seg_ref) + EPS) * qn_ref[...]
    gq_ref[0] = (_rope(gq, cg, sg, GQA_D) * GQA_D ** -0.5).astype(gq_ref.dtype)
    kw = o[7] - o[6]
    gk = proj(o[6], o[7])
    gk = gk * lax.rsqrt(_seg_mean(gk * gk, seg_ref.at[0:kw, 0:kw]) + EPS) * kn_ref[...]
    gk_ref[0] = _rope(gk, cg[:, 0:kw], sg[:, 0:kw], GQA_D).astype(gk_ref.dtype)
    gv_ref[0] = proj(o[7], o[8]).astype(gv_ref.dtype)

    nq_ref[0] = (proj(o[8], o[9]) * NA_D ** -0.5).astype(nq_ref.dtype)
    nk_ref[0] = proj(o[9], o[10]).astype(nk_ref.dtype)
    nv_ref[0] = proj(o[10], o[11]).astype(nv_ref.dtype)

    n_gate = gates_ref.shape[-1]
    for j in range(N_BRANCH):
        w = n_gate // N_BRANCH
        gates_ref[0, :, j * w:(j + 1) * w] = proj(o[11] + j * w, o[11] + (j + 1) * w)


def _in_proj(x, modt, w, b, cd, sd, cg, sg, seg, qn, kn):
    B, T, D = x.shape
    nt = T // TILE
    n_gate = w.shape[1] - _IN_OFFS[11]
    tok = lambda width: pl.BlockSpec((1, TILE, width), lambda bi, i: (bi, i, 0))
    tab = lambda width: pl.BlockSpec((TILE, width), lambda bi, i: (i, 0))
    widths = (512, 256, 256, 256, 256, 128, 128, 256, 256, 256)
    dtypes = (F32,) + (MXU_DTYPE,) * 9
    out_shape = [jax.ShapeDtypeStruct((B, T, wd), dt) for wd, dt in zip(widths, dtypes)]
    out_shape.append(jax.ShapeDtypeStruct((B, T, n_gate), F32))
    return pl.pallas_call(
        _in_kernel,
        grid=(B, nt),
        in_specs=[tok(D),
                  pl.BlockSpec((1, 1, 6 * D), lambda bi, i: (2 * bi + jnp.minimum(i, 1), 0, 0)),
                  _const_spec(w.shape), _const_spec(b.shape),
                  tab(256), tab(256), tab(256), tab(256),
                  _const_spec(seg.shape), _const_spec(qn.shape), _const_spec(kn.shape)],
        out_specs=[tok(wd) for wd in widths] + [tok(n_gate)],
        out_shape=out_shape,
        compiler_params=_params(("parallel", "arbitrary")),
        name="in_proj",
    )(x, modt, w, b, cd, sd, cg, sg, seg, qn, kn)


def _gelu_tanh(x):
    return 0.5 * x * (1.0 + jnp.tanh(0.7978845608028654 * (x + 0.044715 * (x * x * x))))


def _lru_kernel(z_ref, cw_ref, cb_ref, wg_ref, bg_ref, sp_ref, y_ref,
                af_ref, bf_ref, ab_ref, bb_ref, hf_ref, hb_ref, *, n_ctx_tiles):
    T = z_ref.shape[1]
    W = LRU_WIDTH
    nt = T // TILE
    last = nt - 1
    a_refs = (af_ref, ab_ref)
    b_refs = (bf_ref, bb_ref)

    def coeff_tile(c, carry):
        r0 = pl.multiple_of(c * TILE, TILE)
        seg_first = (c == 0) | (c == n_ctx_tiles)
        seg_last = (c == n_ctx_tiles - 1) | (c == last)
        p0 = pl.multiple_of(jnp.maximum(r0 - SUBLANES, 0), SUBLANES)
        n0 = pl.multiple_of(jnp.minimum(r0 + TILE, T - SUBLANES), SUBLANES)
        prev = jnp.where(seg_first, 0.0, z_ref[0, pl.ds(p0, SUBLANES), 0:W])
        nxt = jnp.where(seg_last, 0.0, z_ref[0, pl.ds(n0, SUBLANES), 0:W])
        ext = jnp.concatenate([prev, z_ref[0, pl.ds(r0, TILE), 0:W], nxt], axis=0)
        n_ext = TILE + 2 * SUBLANES
        xc = cb_ref[...]
        for k in range(LRU_CONV):
            off = k - LRU_CONV_LEFT
            sh = ext if off == 0 else pltpu.roll(ext, (-off) % n_ext, 0)
            xc = xc + cw_ref[k:k + 1, :] * sh[SUBLANES:SUBLANES + TILE]
        pre = _mm(xc, wg_ref[...]) + bg_ref[...]
        for d in range(2):
            r = jax.nn.sigmoid(pre[:, 2 * d * W:(2 * d + 1) * W])
            gate_i = jax.nn.sigmoid(pre[:, (2 * d + 1) * W:(2 * d + 2) * W])
            log_a = (-LRU_C) * r * sp_ref[d:d + 1, :]
            a2 = jnp.exp(2.0 * log_a)
            mult = jnp.sqrt((1.0 + a2) * jnp.tanh(-log_a))
            a_refs[d][pl.ds(r0, TILE), :] = jnp.exp(log_a)
            b_refs[d][pl.ds(r0, TILE), :] = mult * gate_i * xc
        return carry

    lax.fori_loop(0, nt, coeff_tile, 0)

    row = lax.broadcasted_iota(jnp.int32, (SCAN_ROWS, W), 0)
    n_chunks = T // SCAN_ROWS
    ctx_chunks = n_ctx_tiles * TILE // SCAN_ROWS

    def scan_step(j, carry):
        h_f, h_b = carry
        jf = pl.multiple_of(j * SCAN_ROWS, SCAN_ROWS)
        a = af_ref[pl.ds(jf, SCAN_ROWS), :]
        b = bf_ref[pl.ds(jf, SCAN_ROWS), :]
        for s in (1, 2, 4):
            m = row >= s
            b = jnp.where(m, a * pltpu.roll(b, s, 0) + b, b)
            a = jnp.where(m, a * pltpu.roll(a, s, 0), a)
        h = b + a * h_f
        hf_ref[pl.ds(jf, SCAN_ROWS), :] = h
        h_f = jnp.broadcast_to(h[SCAN_ROWS - 1:SCAN_ROWS, :], (SCAN_ROWS, W))

        cb = jnp.where(j < ctx_chunks, ctx_chunks - 1 - j, n_chunks - 1 + ctx_chunks - j)
        jb = pl.multiple_of(cb * SCAN_ROWS, SCAN_ROWS)
        a = ab_ref[pl.ds(jb, SCAN_ROWS), :]
        b = bb_ref[pl.ds(jb, SCAN_ROWS), :]
        for s in (1, 2, 4):
            m = row < SCAN_ROWS - s
            b = jnp.where(m, a * pltpu.roll(b, SCAN_ROWS - s, 0) + b, b)
            a = jnp.where(m, a * pltpu.roll(a, SCAN_ROWS - s, 0), a)
        h = b + a * h_b
        hb_ref[pl.ds(jb, SCAN_ROWS), :] = h
        h_b = jnp.broadcast_to(h[0:1, :], (SCAN_ROWS, W))
        return h_f, h_b

    zero = jnp.zeros((SCAN_ROWS, W), F32)
    lax.fori_loop(0, n_chunks, scan_step, (zero, zero))

    def out_tile(c, carry):
        r0 = pl.multiple_of(c * TILE, TILE)
        g = z_ref[0, pl.ds(r0, TILE), W:2 * W]
        y_ref[0, pl.ds(r0, TILE), :] = (hf_ref[pl.ds(r0, TILE), :] + hb_ref[pl.ds(r0, TILE), :]) * _gelu_tanh(g)
        return carry

    lax.fori_loop(0, nt, out_tile, 0)


def _lru(z, cw, cb, wg, bg, sp, n_ctx_tiles):
    B, T, _ = z.shape
    W = LRU_WIDTH
    return pl.pallas_call(
        functools.partial(_lru_kernel, n_ctx_tiles=n_ctx_tiles),
        grid=(B,),
        in_specs=[pl.BlockSpec((1, T, 2 * W), lambda bi: (bi, 0, 0)),
                  _const_spec(cw.shape), _const_spec(cb.shape), _const_spec(wg.shape),
                  _const_spec(bg.shape), _const_spec(sp.shape)],
        out_specs=pl.BlockSpec((1, T, W), lambda bi: (bi, 0, 0)),
        out_shape=jax.ShapeDtypeStruct((B, T, W), F32),
        scratch_shapes=[pltpu.VMEM((T, W), F32) for _ in range(6)],
        compiler_params=_params(("parallel",)),
        name="rglru",
    )(z, cw, cb, wg, bg, sp)


def _head_q(q, lo, hi):
    return jnp.where(_lane_mask(q.shape[-1], lo, hi), q.astype(F32), 0.0).astype(MXU_DTYPE)


def _exp_rows(s):
    mx = jnp.max(s, axis=-1, keepdims=True)
    e = jnp.exp(s - mx)
    return e, jnp.sum(e, axis=-1, keepdims=True)


def _diff_kernel(lam_ref, q_ref, k_ref, v_ref, seg_ref, g_ref, o_ref, *, n_ctx):
    def attend(nk):
        q = q_ref[0]
        k = k_ref[0, 0:nk, :]
        v = v_ref[0, 0:nk, :]
        lam = lam_ref[...]
        width = q.shape[-1]
        out = jnp.zeros((q.shape[0], width), F32)
        for h in range(DIFF_HEADS):
            c0 = 2 * h * DIFF_D
            e1, l1 = _exp_rows(_mm_nt(_head_q(q, c0, c0 + DIFF_D), k))
            e2, l2 = _exp_rows(_mm_nt(_head_q(q, c0 + DIFF_D, c0 + 2 * DIFF_D), k))
            w = e1 * (1.0 / l1) - e2 * (lam / l2)
            o = jnp.dot(w.astype(MXU_DTYPE), v, preferred_element_type=F32)
            out = jnp.where(_lane_mask(width, h * DIFF_V, (h + 1) * DIFF_V), o, out)
        ms = _seg_mean(out * out, seg_ref)
        o_ref[0] = out * lax.rsqrt(ms + EPS) * g_ref[...]

    i = pl.program_id(1)

    @pl.when(i == 0)
    def _():
        attend(n_ctx)

    @pl.when(i > 0)
    def _():
        attend(k_ref.shape[1])


def _diff_attn(lam, q, k, v, seg, g, n_ctx):
    B, T, W = q.shape
    return pl.pallas_call(
        functools.partial(_diff_kernel, n_ctx=n_ctx),
        grid=(B, T // TILE),
        in_specs=[_const_spec(lam.shape),
                  pl.BlockSpec((1, TILE, W), lambda bi, i: (bi, i, 0)),
                  pl.BlockSpec((1, T, W), lambda bi, i: (bi, 0, 0)),
                  pl.BlockSpec((1, T, W), lambda bi, i: (bi, 0, 0)),
                  _const_spec(seg.shape), _const_spec(g.shape)],
        out_specs=pl.BlockSpec((1, TILE, W), lambda bi, i: (bi, i, 0)),
        out_shape=jax.ShapeDtypeStruct((B, T, W), F32),
        compiler_params=_params(("parallel", "arbitrary")),
        name="diff_attn",
    )(lam, q, k, v, seg, g)


def _gqa_kernel(q_ref, k_ref, v_ref, o_ref, *, n_ctx):
    def attend(nk):
        k = k_ref[0, 0:nk, :]
        v = v_ref[0, 0:nk, :]
        kvw = k.shape[-1]
        for half in range(GQA_Q_HEADS // GQA_KV_HEADS):
            q = q_ref[0, :, half * kvw:(half + 1) * kvw]
            out = jnp.zeros((q.shape[0], kvw), F32)
            for j in range(GQA_KV_HEADS):
                e, l = _exp_rows(_mm_nt(_head_q(q, j * GQA_D, (j + 1) * GQA_D), k))
                o = jnp.dot(e.astype(MXU_DTYPE), v, preferred_element_type=F32) * (1.0 / l)
                out = jnp.where(_lane_mask(kvw, j * GQA_D, (j + 1) * GQA_D), o, out)
            o_ref[0, :, half * kvw:(half + 1) * kvw] = out

    i = pl.program_id(1)

    @pl.when(i == 0)
    def _():
        attend(n_ctx)

    @pl.when(i > 0)
    def _():
        attend(k_ref.shape[1])


def _gqa_attn(q, k, v, n_ctx):
    B, T, W = q.shape
    KW = k.shape[-1]
    return pl.pallas_call(
        functools.partial(_gqa_kernel, n_ctx=n_ctx),
        grid=(B, T // TILE),
        in_specs=[pl.BlockSpec((1, TILE, W), lambda bi, i: (bi, i, 0)),
                  pl.BlockSpec((1, T, KW), lambda bi, i: (bi, 0, 0)),
                  pl.BlockSpec((1, T, KW), lambda bi, i: (bi, 0, 0))],
        out_specs=pl.BlockSpec((1, TILE, W), lambda bi, i: (bi, i, 0)),
        out_shape=jax.ShapeDtypeStruct((B, T, W), F32),
        compiler_params=_params(("parallel", "arbitrary")),
        name="gqa_attn",
    )(q, k, v)


def _na_key_row0(i, n_lat_tiles):
    p = i - 1
    return jnp.clip(NA_Q_ROWS * p - NA_WIN_ROWS // 2, 0, n_lat_tiles * NA_Q_ROWS - NA_K_ROWS)


def _na_kernel(q_ref, k_ref, v_ref, bias_ref, o_ref, *, n_ctx):
    T = k_ref.shape[1]
    n_lat_tiles = (T - n_ctx) // TILE
    width = q_ref.shape[-1]
    i = pl.program_id(1)

    @pl.when(i == 0)
    def _():
        q = q_ref[0]
        k = k_ref[0, 0:n_ctx, :]
        v = v_ref[0, 0:n_ctx, :]
        out = jnp.zeros((q.shape[0], width), F32)
        for h in range(NA_HEADS):
            e, l = _exp_rows(_mm_nt(_head_q(q, h * NA_D, (h + 1) * NA_D), k))
            o = jnp.dot(e.astype(MXU_DTYPE), v, preferred_element_type=F32) * (1.0 / l)
            out = jnp.where(_lane_mask(width, h * NA_D, (h + 1) * NA_D), o, out)
        o_ref[0] = out

    @pl.when(i > 0)
    def _():
        q = q_ref[0]
        nkw = NA_K_ROWS * GRID_W
        k0 = pl.multiple_of(n_ctx + _na_key_row0(i, n_lat_tiles) * GRID_W, GRID_W)
        kc = k_ref[0, 0:n_ctx, :]
        vc = v_ref[0, 0:n_ctx, :]
        kw = k_ref[0, pl.ds(k0, nkw), :]
        vw = v_ref[0, pl.ds(k0, nkw), :]
        out = jnp.zeros((q.shape[0], width), F32)
        for h in range(NA_HEADS):
            qh = _head_q(q, h * NA_D, (h + 1) * NA_D)
            s_c = _mm_nt(qh, kc)
            s_w = _mm_nt(qh, kw) + bias_ref[0, h]
            mx = jnp.maximum(jnp.max(s_c, axis=-1, keepdims=True), jnp.max(s_w, axis=-1, keepdims=True))
            e_c = jnp.exp(s_c - mx)
            e_w = jnp.exp(s_w - mx)
            l = jnp.sum(e_c, axis=-1, keepdims=True) + jnp.sum(e_w, axis=-1, keepdims=True)
            o = (jnp.dot(e_c.astype(MXU_DTYPE), vc, preferred_element_type=F32)
                 + jnp.dot(e_w.astype(MXU_DTYPE), vw, preferred_element_type=F32)) * (1.0 / l)
            out = jnp.where(_lane_mask(width, h * NA_D, (h + 1) * NA_D), o, out)
        o_ref[0] = out


def _na_attn(q, k, v, bias, n_ctx):
    B, T, W = q.shape
    nt = T // TILE

    def bias_idx(bi, i):
        kind = jnp.where(i <= 1, 0, jnp.where(i == nt - 1, 2, 1))
        return (kind, 0, 0, 0)

    return pl.pallas_call(
        functools.partial(_na_kernel, n_ctx=n_ctx),
        grid=(B, nt),
        in_specs=[pl.BlockSpec((1, TILE, W), lambda bi, i: (bi, i, 0)),
                  pl.BlockSpec((1, T, W), lambda bi, i: (bi, 0, 0)),
                  pl.BlockSpec((1, T, W), lambda bi, i: (bi, 0, 0)),
                  pl.BlockSpec((1,) + bias.shape[1:], bias_idx)],
        out_specs=pl.BlockSpec((1, TILE, W), lambda bi, i: (bi, i, 0)),
        out_shape=jax.ShapeDtypeStruct((B, T, W), F32),
        compiler_params=_params(("parallel", "arbitrary")),
        name="na_attn",
    )(q, k, v, bias)


def _na_bias_table(rpb, n_lat_rows):
    nq = NA_Q_ROWS * GRID_W
    nk = NA_K_ROWS * GRID_W
    a = (np.arange(nq) // GRID_W)[:, None]
    c = (np.arange(nq) % GRID_W)[:, None]
    j = (np.arange(nk) // GRID_W)[None, :]
    c2 = (np.arange(nk) % GRID_W)[None, :]
    c0 = np.clip(c - NA_WIN_COLS // 2, 0, GRID_W - NA_WIN_COLS)
    col_ok = (c2 >= c0) & (c2 < c0 + NA_WIN_COLS)
    cidx = np.clip(c2 - c + NA_WIN_COLS - 1, 0, 2 * NA_WIN_COLS - 2)
    half = NA_WIN_ROWS // 2
    last_off = n_lat_rows - NA_WIN_ROWS - (n_lat_rows - NA_K_ROWS)
    last_rel = (n_lat_rows - NA_Q_ROWS) - (n_lat_rows - NA_K_ROWS)
    tables = []
    for rel, lo in ((0, 0 * a), (half, a), (last_rel, last_off + 0 * a)):
        row_ok = (j >= lo) & (j < lo + NA_WIN_ROWS)
        ridx = np.clip(j - rel - a + NA_WIN_ROWS - 1, 0, 2 * NA_WIN_ROWS - 2)
        ridx_b, cidx_b = np.broadcast_arrays(ridx, cidx)
        vals = rpb[:, ridx_b, cidx_b]
        tables.append(jnp.where(jnp.asarray(row_ok & col_ok)[None], vals, NEG_BIG))
    return jnp.stack(tables, axis=0)


def _merge_kernel(ya_ref, yb_ref, yc_ref, yd_ref, gates_ref, x_ref, mod_ref, wb_ref, wo_ref, bo_ref,
                  g_ref, b_ref, o_ref):
    D = x_ref.shape[-1]
    m = jnp.zeros((x_ref.shape[1], D), F32)
    for j, y_ref in enumerate((ya_ref, yb_ref, yc_ref, yd_ref)):
        gate = jax.nn.sigmoid(gates_ref[0, :, j * D:(j + 1) * D])
        m = m + gate * _mm(y_ref[0], wb_ref[j])
    out = _mm(m, wo_ref[...]) + bo_ref[...]
    g1 = mod_ref[0][:, 2 * D:3 * D]
    o_ref[0] = _ln(ALPHA * x_ref[0] + g1 * out) * g_ref[...] + b_ref[...]


def _merge(ys, gates, x, modt, wb, wo, bo, g, b, skip_tiles):
    B, T, D = x.shape
    nt = T // TILE - skip_tiles
    tok = lambda width: pl.BlockSpec((1, TILE, width), lambda bi, i: (bi, i + skip_tiles, 0))
    return pl.pallas_call(
        _merge_kernel,
        grid=(B, nt),
        in_specs=[tok(BRANCH_W)] * 4 + [tok(gates.shape[-1]), tok(D),
                  pl.BlockSpec((1, 1, 6 * D), lambda bi, i: (2 * bi + jnp.minimum(i + skip_tiles, 1), 0, 0)),
                  _const_spec(wb.shape), _const_spec(wo.shape), _const_spec(bo.shape),
                  _const_spec(g.shape), _const_spec(b.shape)],
        out_specs=pl.BlockSpec((1, TILE, D), lambda bi, i: (bi, i, 0)),
        out_shape=jax.ShapeDtypeStruct((B, nt * TILE, D), F32),
        compiler_params=_params(("parallel", "arbitrary")),
        name="merge",
    )(*ys, gates, x, modt, wb, wo, bo, g, b)


def _ffn_kernel(xp_ref, x_ref, xn_ref, mod_ref, w1_ref, b1_ref, cw_ref, cb_ref, w2_ref, b2_ref,
                g_ref, b_ref, o_ref, *, seq_first, seq_last):
    D = x_ref.shape[-1]
    d_ff = w2_ref.shape[0]
    i = pl.program_id(1)
    has_prev = functools.reduce(jnp.logical_and, [i != t for t in seq_first])
    has_next = functools.reduce(jnp.logical_and, [i != t for t in seq_last])
    mod = mod_ref[0]
    x = x_ref[0]
    n_ext = TILE + 2 * SUBLANES
    xe = jnp.concatenate([xp_ref[0], x, xn_ref[0]], axis=0)
    h = (_ln(xe) * (1.0 + mod[:, 4 * D:5 * D]) + mod[:, 3 * D:4 * D]).astype(MXU_DTYPE)
    row = lax.broadcasted_iota(jnp.int32, (n_ext, 1), 0)
    inside = ((row >= SUBLANES) | has_prev) & ((row < SUBLANES + TILE) | has_next)

    def conv(lo, hi):
        u = jnp.dot(h, w1_ref[:, lo:hi], preferred_element_type=F32) + b1_ref[:, lo:hi]
        u = jnp.where(inside, u, 0.0)
        y = (cw_ref[0:1, lo:hi] * pltpu.roll(u, 1, 0)[SUBLANES:SUBLANES + TILE]
             + cw_ref[1:2, lo:hi] * u[SUBLANES:SUBLANES + TILE]
             + cw_ref[2:3, lo:hi] * pltpu.roll(u, n_ext - 1, 0)[SUBLANES:SUBLANES + TILE])
        return y + cb_ref[:, lo:hi]

    acc = jnp.zeros((TILE, D), F32)
    for lo in range(0, d_ff, FF_CHUNK):
        hi = min(lo + FF_CHUNK, d_ff)
        val = conv(lo, hi)
        gate = conv(d_ff + lo, d_ff + hi)
        act = gate * jax.nn.sigmoid(gate) * val
        acc = acc + _mm(act, w2_ref[lo:hi, :])
    f = acc + b2_ref[...]
    g2 = mod[:, 5 * D:6 * D]
    o_ref[0] = _ln(ALPHA * x + g2 * f) * g_ref[...] + b_ref[...]


def _ffn(x, modt, w1, b1, cw, cb, w2, b2, g, b, n_ctx_tiles, latent_only):
    B, T, D = x.shape
    nt = T // TILE
    rows8 = T // SUBLANES
    per = TILE // SUBLANES
    if latent_only:
        seq_first, seq_last = (0,), (nt - 1,)
        mod_idx = lambda bi, i: (2 * bi + 1, 0, 0)
    else:
        seq_first, seq_last = (0, n_ctx_tiles), (n_ctx_tiles - 1, nt - 1)
        mod_idx = lambda bi, i: (2 * bi + jnp.minimum(i, 1), 0, 0)
    return pl.pallas_call(
        functools.partial(_ffn_kernel, seq_first=seq_first, seq_last=seq_last),
        grid=(B, nt),
        in_specs=[pl.BlockSpec((1, SUBLANES, D), lambda bi, i: (bi, jnp.maximum(i * per - 1, 0), 0)),
                  pl.BlockSpec((1, TILE, D), lambda bi, i: (bi, i, 0)),
                  pl.BlockSpec((1, SUBLANES, D), lambda bi, i: (bi, jnp.minimum((i + 1) * per, rows8 - 1), 0)),
                  pl.BlockSpec((1, 1, 6 * D), mod_idx),
                  _const_spec(w1.shape), _const_spec(b1.shape), _const_spec(cw.shape),
                  _const_spec(cb.shape), _const_spec(w2.shape), _const_spec(b2.shape),
                  _const_spec(g.shape), _const_spec(b.shape)],
        out_specs=pl.BlockSpec((1, TILE, D), lambda bi, i: (bi, i, 0)),
        out_shape=jax.ShapeDtypeStruct((B, T, D), F32),
        compiler_params=_params(("parallel", "arbitrary")),
        name="conv_ffn",
    )(x, x, x, modt, w1, b1, cw, cb, w2, b2, g, b)


def _rope_tables(n_lat, n_ctx, dim, width):
    t = jnp.arange(n_lat)
    row = (t // GRID_W).astype(F32)
    col = (t % GRID_W).astype(F32)
    n_freq = dim // 4
    inv = ROPE_BASE ** (-jnp.arange(n_freq, dtype=F32) / n_freq)
    ang = jnp.concatenate([row[:, None] * inv, col[:, None] * inv], -1)
    cos, sin = jnp.cos(ang), jnp.sin(ang)
    cos = jnp.tile(jnp.concatenate([cos, cos], -1), (1, width // dim))
    sin = jnp.tile(jnp.concatenate([-sin, sin], -1), (1, width // dim))
    cos = jnp.concatenate([jnp.ones((n_ctx, width), F32), cos], 0)
    sin = jnp.concatenate([jnp.zeros((n_ctx, width), F32), sin], 0)
    return cos, sin


def _block_diag(w):
    nb, bw, _ = w.shape
    return jnp.einsum('ncd,nm->ncmd', w, jnp.eye(nb, dtype=w.dtype)).reshape(nb * bw, nb * bw)


_GQA_HEAD_ORDER = (0, 2, 1, 3)


def _permute_heads(a, axis, order, d):
    shp = a.shape
    a = a.reshape(shp[:axis] + (len(order), d) + shp[axis + 1:])
    a = jnp.take(a, jnp.asarray(order), axis=axis)
    return a.reshape(shp)


def kernel(x, c, ctx, c_ctx, w_ada, b_ada, w_in, b_in, lru_conv_w, lru_conv_b, lru_w_a, lru_b_a, lru_w_x, lru_b_x, lru_lambda, diff_lambda, diff_subln, gqa_q_norm, gqa_k_norm, na_rpb, w_branch, w_out, b_out, ln1_g, ln1_b, ffn_w1, ffn_b1, ffn_conv_w, ffn_conv_b, ffn_w2, ffn_b2, ln2_g, ln2_b):
    B, S, D = x.shape
    C = ctx.shape[1]
    L = w_ada.shape[0]
    assert C == TILE and S % TILE == 0 and S // GRID_W >= NA_K_ROWS
    n_ctx_tiles = C // TILE

    pad = (-(B + 1)) % SUBLANES
    cvec = jnp.concatenate([c, c_ctx[None], jnp.zeros((pad, D), F32)], 0)
    mod_all = _ada(cvec, w_ada, b_ada)
    mod_lat = mod_all[:, :B]
    mod_ctx = jnp.broadcast_to(mod_all[:, B:B + 1], mod_lat.shape)
    modt = jnp.stack([mod_ctx, mod_lat], axis=2).reshape(L, 2 * B, 1, 6 * D)

    cd, sd = _rope_tables(S, C, DIFF_D, 256)
    cg, sg = _rope_tables(S, C, GQA_D, 256)
    seg64 = jnp.asarray(np.kron(np.eye(4), np.full((64, 64), 1.0 / 64)), MXU_DTYPE)

    o = _IN_OFFS
    gq_cols = _permute_heads(w_in[:, :, o[5]:o[6]], 2, _GQA_HEAD_ORDER, GQA_D)
    w_in_k = jnp.concatenate([w_in[:, :, :o[5]], gq_cols, w_in[:, :, o[6]:]], 2).astype(MXU_DTYPE)
    gq_b = _permute_heads(b_in[:, o[5]:o[6]], 1, _GQA_HEAD_ORDER, GQA_D)
    b_in_k = jnp.concatenate([b_in[:, :o[5]], gq_b, b_in[:, o[6]:]], 1)[:, None, :]
    wb_k = jnp.concatenate([w_branch[:, :2], _permute_heads(w_branch[:, 2:3], 2, _GQA_HEAD_ORDER, GQA_D),
                            w_branch[:, 3:]], 1).astype(MXU_DTYPE)
    wo_k = w_out.astype(MXU_DTYPE)
    w1_k = ffn_w1.astype(MXU_DTYPE)
    w2_k = ffn_w2.astype(MXU_DTYPE)
    qn = jnp.tile(gqa_q_norm, (1, GQA_Q_HEADS))[:, None, :]
    kn = jnp.tile(gqa_k_norm, (1, GQA_KV_HEADS))[:, None, :]
    wg = jnp.concatenate([jax.vmap(_block_diag)(lru_w_a[:, 0]), jax.vmap(_block_diag)(lru_w_x[:, 0]),
                          jax.vmap(_block_diag)(lru_w_a[:, 1]), jax.vmap(_block_diag)(lru_w_x[:, 1])],
                         -1).astype(MXU_DTYPE)
    bg = jnp.concatenate([lru_b_a[:, 0].reshape(L, -1), lru_b_x[:, 0].reshape(L, -1),
                          lru_b_a[:, 1].reshape(L, -1), lru_b_x[:, 1].reshape(L, -1)], -1)[:, None, :]
    sp = jax.nn.softplus(-lru_lambda.astype(F32))
    lp = diff_lambda.astype(F32)
    lam_delta = jnp.exp(jnp.sum(lp[:, 0] * lp[:, 1], -1)) - jnp.exp(jnp.sum(lp[:, 2] * lp[:, 3], -1))

    xs = jnp.concatenate([ctx, x], axis=1)
    for l in range(L):
        last = l == L - 1
        lam_init = 0.8 - 0.6 * math.exp(-0.3 * l)
        lam = (lam_delta[l] + lam_init).reshape(1, 1)
        subln = (jnp.tile(diff_subln[l], DIFF_HEADS) * (1.0 - lam_init))[None, :]
        bias = _na_bias_table(na_rpb[l], S // GRID_W)

        (z_lru, dq, dk, dv, gq, gk, gv, nq, nk, nv, gates) = _in_proj(
            xs, modt[l], w_in_k[l], b_in_k[l], cd, sd, cg, sg, seg64, qn[l], kn[l])
        y_a = _lru(z_lru, lru_conv_w[l], lru_conv_b[l][None, :], wg[l], bg[l], sp[l], n_ctx_tiles)
        y_b = _diff_attn(lam, dq, dk, dv, seg64, subln, C)
        y_c = _gqa_attn(gq, gk, gv, C)
        y_d = _na_attn(nq, nk, nv, bias, C)

        skip = n_ctx_tiles if last else 0
        x1 = _merge((y_a, y_b, y_c, y_d), gates, xs, modt[l], wb_k[l], wo_k[l], b_out[l][None, :],
                    ln1_g[l][None, :], ln1_b[l][None, :], skip)
        xs = _ffn(x1, modt[l], w1_k[l], ffn_b1[l][None, :], ffn_conv_w[l], ffn_conv_b[l][None, :],
                  w2_k[l], ffn_b2[l][None, :], ln2_g[l][None, :], ln2_b[l][None, :], n_ctx_tiles, last)
    return xs
```

```python
import functools
import math

import numpy as np
import jax
import jax.numpy as jnp
from jax import lax
from jax.experimental import pallas as pl
from jax.experimental.pallas import tpu as pltpu

F32 = jnp.float32
MXU_DTYPE = jnp.bfloat16

DEPTH = 4
GRID_W = 64
LRU_WIDTH = 256
LRU_BLOCKS = 4
LRU_C = 8.0
LRU_CONV = 4
LRU_CONV_LEFT = 2
DIFF_HEADS = 4
DIFF_D = 32
DIFF_V = 64
GQA_Q_HEADS = 4
GQA_KV_HEADS = 2
GQA_D = 64
NA_HEADS = 4
NA_D = 64
NA_WIN_ROWS = 8
NA_WIN_COLS = 16
N_BRANCH = 4
BRANCH_W = 256
FFN_CONV = 3
ROPE_BASE = 10000.0
ALPHA = (2 * DEPTH) ** 0.25
EPS = 1e-6
NEG_BIG = -1e30

TILE = 256
SUBLANES = 8
NA_Q_ROWS = TILE // GRID_W
NA_K_ROWS = 12
FF_CHUNK = 768
SCAN_ROWS = 8
V7X_VMEM_LIMIT = 56 * 1024 * 1024

_IN_WIDTHS = (256, 256, 256, 256, 256, 256, 128, 128, 256, 256, 256)
_IN_OFFS = tuple(int(v) for v in np.cumsum((0,) + _IN_WIDTHS))


def _params(sem):
    return pltpu.CompilerParams(dimension_semantics=sem, vmem_limit_bytes=V7X_VMEM_LIMIT)


def _const_spec(shape):
    nd = len(shape)
    return pl.BlockSpec(shape, lambda *_: (0,) * nd)


def _ln(x):
    mu = jnp.mean(x, axis=-1, keepdims=True)
    xc = x - mu
    var = jnp.mean(xc * xc, axis=-1, keepdims=True)
    return xc * lax.rsqrt(var + EPS)


def _mm(a, b):
    return jnp.dot(a.astype(MXU_DTYPE), b, preferred_element_type=F32)


def _mm_nt(a, b):
    return lax.dot_general(a, b, (((1,), (1,)), ((), ())), preferred_element_type=F32)


def _seg_mean(x2, seg_ref):
    hi = x2.astype(MXU_DTYPE)
    lo = (x2 - hi.astype(F32)).astype(MXU_DTYPE)
    seg = seg_ref[...]
    return (jnp.dot(hi, seg, preferred_element_type=F32)
            + jnp.dot(lo, seg, preferred_element_type=F32))


def _lane_mask(width, lo, hi):
    lane = lax.broadcasted_iota(jnp.int32, (1, width), 1)
    return (lane >= lo) & (lane < hi)


def _rope(x, cos, sin_signed, dim):
    n = x.shape[-1]
    half = dim // 2
    lane = lax.broadcasted_iota(jnp.int32, (1, n), 1)
    first = (lane & (dim - 1)) < half
    swapped = jnp.where(first, pltpu.roll(x, n - half, 1), pltpu.roll(x, half, 1))
    return x * cos + swapped * sin_signed


def _ada_kernel(c_ref, w_ref, b_ref, o_ref):
    c = c_ref[...]
    s = c * jax.nn.sigmoid(c)
    o_ref[0] = _mm(s, w_ref[0].astype(MXU_DTYPE)) + b_ref[0]


def _ada(cvec, w_ada, b_ada):
    L, D, N = w_ada.shape
    tn = 1536
    R = cvec.shape[0]
    return pl.pallas_call(
        _ada_kernel,
        grid=(L, N // tn),
        in_specs=[pl.BlockSpec((R, D), lambda l, j: (0, 0)),
                  pl.BlockSpec((1, D, tn), lambda l, j: (l, 0, j)),
                  pl.BlockSpec((1, 1, tn), lambda l, j: (l, 0, j))],
        out_specs=pl.BlockSpec((1, R, tn), lambda l, j: (l, 0, j)),
        out_shape=jax.ShapeDtypeStruct((L, R, N), F32),
        compiler_params=_params(("parallel", "parallel")),
        name="ada_mod",
    )(cvec, w_ada, b_ada.reshape(L, 1, N))


def _in_kernel(x_ref, mod_ref, w_ref, b_ref, cd_ref, sd_ref, cg_ref, sg_ref, seg_ref, qn_ref, kn_ref,
               lru_ref, dq_ref, dk_ref, dv_ref, gq_ref, gk_ref, gv_ref, nq_ref, nk_ref, nv_ref,
               gates_ref):
    D = x_ref.shape[-1]
    mod = mod_ref[0]
    h = (_ln(x_ref[0]) * (1.0 + mod[:, D:2 * D]) + mod[:, 0:D]).astype(MXU_DTYPE)

    def proj(lo, hi):
        return jnp.dot(h, w_ref[:, lo:hi], preferred_element_type=F32) + b_ref[:, lo:hi]

    o = _IN_OFFS
    lru_ref[0] = proj(o[0], o[2])

    cd, sd = cd_ref[...], sd_ref[...]
    dq_ref[0] = (_rope(proj(o[2], o[3]), cd, sd, DIFF_D) * DIFF_D ** -0.5).astype(dq_ref.dtype)
    dk_ref[0] = _rope(proj(o[3], o[4]), cd, sd, DIFF_D).astype(dk_ref.dtype)
    dv_ref[0] = proj(o[4], o[5]).astype(dv_ref.dtype)

    cg, sg = cg_ref[...], sg_ref[...]
    gq = proj(o[5], o[6])
    gq = gq * lax.rsqrt(_seg_mean(gq * gq, seg_ref) + EPS) * qn_ref[...]
    gq_ref[0] = (_rope(gq, cg, sg, GQA_D) * GQA_D ** -0.5).astype(gq_ref.dtype)
    kw = o[7] - o[6]
    gk = proj(o[6], o[7])
    gk = gk * lax.rsqrt(_seg_mean(gk * gk, seg_ref.at[0:kw, 0:kw]) + EPS) * kn_ref[...]
    gk_ref[0] = _rope(gk, cg[:, 0:kw], sg[:, 0:kw], GQA_D).astype(gk_ref.dtype)
    gv_ref[0] = proj(o[7], o[8]).astype(gv_ref.dtype)

    nq_ref[0] = (proj(o[8], o[9]) * NA_D ** -0.5).astype(nq_ref.dtype)
    nk_ref[0] = proj(o[9], o[10]).astype(nk_ref.dtype)
    nv_ref[0] = proj(o[10], o[11]).astype(nv_ref.dtype)

    n_gate = gates_ref.shape[-1]
    for j in range(N_BRANCH):
        w = n_gate // N_BRANCH
        gates_ref[0, :, j * w:(j + 1) * w] = proj(o[11] + j * w, o[11] + (j + 1) * w)


def _in_proj(x, modt, w, b, cd, sd, cg, sg, seg, qn, kn):
    B, T, D = x.shape
    nt = T // TILE
    n_gate = w.shape[1] - _IN_OFFS[11]
    tok = lambda width: pl.BlockSpec((1, TILE, width), lambda bi, i: (bi, i, 0))
    tab = lambda width: pl.BlockSpec((TILE, width), lambda bi, i: (i, 0))
    widths = (512, 256, 256, 256, 256, 128, 128, 256, 256, 256)
    dtypes = (F32,) + (MXU_DTYPE,) * 9
    out_shape = [jax.ShapeDtypeStruct((B, T, wd), dt) for wd, dt in zip(widths, dtypes)]
    out_shape.append(jax.ShapeDtypeStruct((B, T, n_gate), F32))
    return pl.pallas_call(
        _in_kernel,
        grid=(B, nt),
        in_specs=[tok(D),
                  pl.BlockSpec((1, 1, 6 * D), lambda bi, i: (2 * bi + jnp.minimum(i, 1), 0, 0)),
                  _const_spec(w.shape), _const_spec(b.shape),
                  tab(256), tab(256), tab(256), tab(256),
                  _const_spec(seg.shape), _const_spec(qn.shape), _const_spec(kn.shape)],
        out_specs=[tok(wd) for wd in widths] + [tok(n_gate)],
        out_shape=out_shape,
        compiler_params=_params(("parallel", "arbitrary")),
        name="in_proj",
    )(x, modt, w, b, cd, sd, cg, sg, seg, qn, kn)


def _gelu_tanh(x):
    return 0.5 * x * (1.0 + jnp.tanh(0.7978845608028654 * (x + 0.044715 * (x * x * x))))


def _lru_kernel(z_ref, cw_ref, cb_ref, wg_ref, bg_ref, sp_ref, y_ref,
                af_ref, bf_ref, ab_ref, bb_ref, hf_ref, hb_ref, *, n_ctx_tiles):
    T = z_ref.shape[1]
    W = LRU_WIDTH
    nt = T // TILE
    last = nt - 1
    a_refs = (af_ref, ab_ref)
    b_refs = (bf_ref, bb_ref)

    def coeff_tile(c, carry):
        r0 = pl.multiple_of(c * TILE, TILE)
        seg_first = (c == 0) | (c == n_ctx_tiles)
        seg_last = (c == n_ctx_tiles - 1) | (c == last)
        p0 = pl.multiple_of(jnp.maximum(r0 - SUBLANES, 0), SUBLANES)
        n0 = pl.multiple_of(jnp.minimum(r0 + TILE, T - SUBLANES), SUBLANES)
        prev = jnp.where(seg_first, 0.0, z_ref[0, pl.ds(p0, SUBLANES), 0:W])
        nxt = jnp.where(seg_last, 0.0, z_ref[0, pl.ds(n0, SUBLANES), 0:W])
        ext = jnp.concatenate([prev, z_ref[0, pl.ds(r0, TILE), 0:W], nxt], axis=0)
        n_ext = TILE + 2 * SUBLANES
        xc = cb_ref[...]
        for k in range(LRU_CONV):
            off = k - LRU_CONV_LEFT
            sh = ext if off == 0 else pltpu.roll(ext, (-off) % n_ext, 0)
            xc = xc + cw_ref[k:k + 1, :] * sh[SUBLANES:SUBLANES + TILE]
        pre = _mm(xc, wg_ref[...]) + bg_ref[...]
        for d in range(2):
            r = jax.nn.sigmoid(pre[:, 2 * d * W:(2 * d + 1) * W])
            gate_i = jax.nn.sigmoid(pre[:, (2 * d + 1) * W:(2 * d + 2) * W])
            log_a = (-LRU_C) * r * sp_ref[d:d + 1, :]
            a2 = jnp.exp(2.0 * log_a)
            mult = jnp.sqrt((1.0 + a2) * jnp.tanh(-log_a))
            a_refs[d][pl.ds(r0, TILE), :] = jnp.exp(log_a)
            b_refs[d][pl.ds(r0, TILE), :] = mult * gate_i * xc
        return carry

    lax.fori_loop(0, nt, coeff_tile, 0)

    row = lax.broadcasted_iota(jnp.int32, (SCAN_ROWS, W), 0)
    n_chunks = T // SCAN_ROWS
    ctx_chunks = n_ctx_tiles * TILE // SCAN_ROWS

    def scan_step(j, carry):
        h_f, h_b = carry
        jf = pl.multiple_of(j * SCAN_ROWS, SCAN_ROWS)
        a = af_ref[pl.ds(jf, SCAN_ROWS), :]
        b = bf_ref[pl.ds(jf, SCAN_ROWS), :]
        for s in (1, 2, 4):
            m = row >= s
            b = jnp.where(m, a * pltpu.roll(b, s, 0) + b, b)
            a = jnp.where(m, a * pltpu.roll(a, s, 0), a)
        h = b + a * h_f
        hf_ref[pl.ds(jf, SCAN_ROWS), :] = h
        h_f = jnp.broadcast_to(h[SCAN_ROWS - 1:SCAN_ROWS, :], (SCAN_ROWS, W))

        cb = jnp.where(j < ctx_chunks, ctx_chunks - 1 - j, n_chunks - 1 + ctx_chunks - j)
        jb = pl.multiple_of(cb * SCAN_ROWS, SCAN_ROWS)
        a = ab_ref[pl.ds(jb, SCAN_ROWS), :]
        b = bb_ref[pl.ds(jb, SCAN_ROWS), :]
        for s in (1, 2, 4):
            m = row < SCAN_ROWS - s
            b = jnp.where(m, a * pltpu.roll(b, SCAN_ROWS - s, 0) + b, b)
            a = jnp.where(m, a * pltpu.roll(a, SCAN_ROWS - s, 0), a)
        h = b + a * h_b
        hb_ref[pl.ds(jb, SCAN_ROWS), :] = h
        h_b = jnp.broadcast_to(h[0:1, :], (SCAN_ROWS, W))
        return h_f, h_b

    zero = jnp.zeros((SCAN_ROWS, W), F32)
    lax.fori_loop(0, n_chunks, scan_step, (zero, zero))

    def out_tile(c, carry):
        r0 = pl.multiple_of(c * TILE, TILE)
        g = z_ref[0, pl.ds(r0, TILE), W:2 * W]
        y_ref[0, pl.ds(r0, TILE), :] = (hf_ref[pl.ds(r0, TILE), :] + hb_ref[pl.ds(r0, TILE), :]) * _gelu_tanh(g)
        return carry

    lax.fori_loop(0, nt, out_tile, 0)


def _lru(z, cw, cb, wg, bg, sp, n_ctx_tiles):
    B, T, _ = z.shape
    W = LRU_WIDTH
    return pl.pallas_call(
        functools.partial(_lru_kernel, n_ctx_tiles=n_ctx_tiles),
        grid=(B,),
        in_specs=[pl.BlockSpec((1, T, 2 * W), lambda bi: (bi, 0, 0)),
                  _const_spec(cw.shape), _const_spec(cb.shape), _const_spec(wg.shape),
                  _const_spec(bg.shape), _const_spec(sp.shape)],
        out_specs=pl.BlockSpec((1, T, W), lambda bi: (bi, 0, 0)),
        out_shape=jax.ShapeDtypeStruct((B, T, W), F32),
        scratch_shapes=[pltpu.VMEM((T, W), F32) for _ in range(6)],
        compiler_params=_params(("parallel",)),
        name="rglru",
    )(z, cw, cb, wg, bg, sp)


def _head_q(q, lo, hi):
    return jnp.where(_lane_mask(q.shape[-1], lo, hi), q.astype(F32), 0.0).astype(MXU_DTYPE)


def _exp_rows(s):
    mx = jnp.max(s, axis=-1, keepdims=True)
    e = jnp.exp(s - mx)
    return e, jnp.sum(e, axis=-1, keepdims=True)


def _diff_kernel(lam_ref, q_ref, k_ref, v_ref, seg_ref, g_ref, o_ref, *, n_ctx):
    def attend(nk):
        q = q_ref[0]
        k = k_ref[0, 0:nk, :]
        v = v_ref[0, 0:nk, :]
        lam = lam_ref[...]
        width = q.shape[-1]
        out = jnp.zeros((q.shape[0], width), F32)
        for h in range(DIFF_HEADS):
            c0 = 2 * h * DIFF_D
            e1, l1 = _exp_rows(_mm_nt(_head_q(q, c0, c0 + DIFF_D), k))
            e2, l2 = _exp_rows(_mm_nt(_head_q(q, c0 + DIFF_D, c0 + 2 * DIFF_D), k))
            w = e1 * (1.0 / l1) - e2 * (lam / l2)
            o = jnp.dot(w.astype(MXU_DTYPE), v, preferred_element_type=F32)
            out = jnp.where(_lane_mask(width, h * DIFF_V, (h + 1) * DIFF_V), o, out)
        ms = _seg_mean(out * out, seg_ref)
        o_ref[0] = out * lax.rsqrt(ms + EPS) * g_ref[...]

    i = pl.program_id(1)

    @pl.when(i == 0)
    def _():
        attend(n_ctx)

    @pl.when(i > 0)
    def _():
        attend(k_ref.shape[1])


def _diff_attn(lam, q, k, v, seg, g, n_ctx):
    B, T, W = q.shape
    return pl.pallas_call(
        functools.partial(_diff_kernel, n_ctx=n_ctx),
        grid=(B, T // TILE),
        in_specs=[_const_spec(lam.shape),
                  pl.BlockSpec((1, TILE, W), lambda bi, i: (bi, i, 0)),
                  pl.BlockSpec((1, T, W), lambda bi, i: (bi, 0, 0)),
                  pl.BlockSpec((1, T, W), lambda bi, i: (bi, 0, 0)),
                  _const_spec(seg.shape), _const_spec(g.shape)],
        out_specs=pl.BlockSpec((1, TILE, W), lambda bi, i: (bi, i, 0)),
        out_shape=jax.ShapeDtypeStruct((B, T, W), F32),
        compiler_params=_params(("parallel", "arbitrary")),
        name="diff_attn",
    )(lam, q, k, v, seg, g)


def _gqa_kernel(q_ref, k_ref, v_ref, o_ref, *, n_ctx):
    def attend(nk):
        k = k_ref[0, 0:nk, :]
        v = v_ref[0, 0:nk, :]
        kvw = k.shape[-1]
        for half in range(GQA_Q_HEADS // GQA_KV_HEADS):
            q = q_ref[0, :, half * kvw:(half + 1) * kvw]
            out = jnp.zeros((q.shape[0], kvw), F32)
            for j in range(GQA_KV_HEADS):
                e, l = _exp_rows(_mm_nt(_head_q(q, j * GQA_D, (j + 1) * GQA_D), k))
                o = jnp.dot(e.astype(MXU_DTYPE), v, preferred_element_type=F32) * (1.0 / l)
                out = jnp.where(_lane_mask(kvw, j * GQA_D, (j + 1) * GQA_D), o, out)
            o_ref[0, :, half * kvw:(half + 1) * kvw] = out

    i = pl.program_id(1)

    @pl.when(i == 0)
    def _():
        attend(n_ctx)

    @pl.when(i > 0)
    def _():
        attend(k_ref.shape[1])


def _gqa_attn(q, k, v, n_ctx):
    B, T, W = q.shape
    KW = k.shape[-1]
    return pl.pallas_call(
        functools.partial(_gqa_kernel, n_ctx=n_ctx),
        grid=(B, T // TILE),
        in_specs=[pl.BlockSpec((1, TILE, W), lambda bi, i: (bi, i, 0)),
                  pl.BlockSpec((1, T, KW), lambda bi, i: (bi, 0, 0)),
                  pl.BlockSpec((1, T, KW), lambda bi, i: (bi, 0, 0))],
        out_specs=pl.BlockSpec((1, TILE, W), lambda bi, i: (bi, i, 0)),
        out_shape=jax.ShapeDtypeStruct((B, T, W), F32),
        compiler_params=_params(("parallel", "arbitrary")),
        name="gqa_attn",
    )(q, k, v)


def _na_key_row0(i, n_lat_tiles):
    p = i - 1
    return jnp.clip(NA_Q_ROWS * p - NA_WIN_ROWS // 2, 0, n_lat_tiles * NA_Q_ROWS - NA_K_ROWS)


def _na_kernel(q_ref, k_ref, v_ref, bias_ref, o_ref, *, n_ctx):
    T = k_ref.shape[1]
    n_lat_tiles = (T - n_ctx) // TILE
    width = q_ref.shape[-1]
    i = pl.program_id(1)

    @pl.when(i == 0)
    def _():
        q = q_ref[0]
        k = k_ref[0, 0:n_ctx, :]
        v = v_ref[0, 0:n_ctx, :]
        out = jnp.zeros((q.shape[0], width), F32)
        for h in range(NA_HEADS):
            e, l = _exp_rows(_mm_nt(_head_q(q, h * NA_D, (h + 1) * NA_D), k))
            o = jnp.dot(e.astype(MXU_DTYPE), v, preferred_element_type=F32) * (1.0 / l)
            out = jnp.where(_lane_mask(width, h * NA_D, (h + 1) * NA_D), o, out)
        o_ref[0] = out

    @pl.when(i > 0)
    def _():
        q = q_ref[0]
        nkw = NA_K_ROWS * GRID_W
        k0 = pl.multiple_of(n_ctx + _na_key_row0(i, n_lat_tiles) * GRID_W, GRID_W)
        kc = k_ref[0, 0:n_ctx, :]
        vc = v_ref[0, 0:n_ctx, :]
        kw = k_ref[0, pl.ds(k0, nkw), :]
        vw = v_ref[0, pl.ds(k0, nkw), :]
        out = jnp.zeros((q.shape[0], width), F32)
        for h in range(NA_HEADS):
            qh = _head_q(q, h * NA_D, (h + 1) * NA_D)
            s_c = _mm_nt(qh, kc)
            s_w = _mm_nt(qh, kw) + bias_ref[0, h]
            mx = jnp.maximum(jnp.max(s_c, axis=-1, keepdims=True), jnp.max(s_w, axis=-1, keepdims=True))
            e_c = jnp.exp(s_c - mx)
            e_w = jnp.exp(s_w - mx)
            l = jnp.sum(e_c, axis=-1, keepdims=True) + jnp.sum(e_w, axis=-1, keepdims=True)
            o = (jnp.dot(e_c.astype(MXU_DTYPE), vc, preferred_element_type=F32)
                 + jnp.dot(e_w.astype(MXU_DTYPE), vw, preferred_element_type=F32)) * (1.0 / l)
            out = jnp.where(_lane_mask(width, h * NA_D, (h + 1) * NA_D), o, out)
        o_ref[0] = out


def _na_attn(q, k, v, bias, n_ctx):
    B, T, W = q.shape
    nt = T // TILE

    def bias_idx(bi, i):
        kind = jnp.where(i <= 1, 0, jnp.where(i == nt - 1, 2, 1))
        return (kind, 0, 0, 0)

    return pl.pallas_call(
        functools.partial(_na_kernel, n_ctx=n_ctx),
        grid=(B, nt),
        in_specs=[pl.BlockSpec((1, TILE, W), lambda bi, i: (bi, i, 0)),
                  pl.BlockSpec((1, T, W), lambda bi, i: (bi, 0, 0)),
                  pl.BlockSpec((1, T, W), lambda bi, i: (bi, 0, 0)),
                  pl.BlockSpec((1,) + bias.shape[1:], bias_idx)],
        out_specs=pl.BlockSpec((1, TILE, W), lambda bi, i: (bi, i, 0)),
        out_shape=jax.ShapeDtypeStruct((B, T, W), F32),
        compiler_params=_params(("parallel", "arbitrary")),
        name="na_attn",
    )(q, k, v, bias)


def _na_bias_table(rpb, n_lat_rows):
    L, H, n_dr, n_dc = rpb.shape
    a = np.arange(NA_Q_ROWS)[:, None]
    j = np.arange(NA_K_ROWS)[None, :]
    c = np.arange(GRID_W)[:, None]
    c2 = np.arange(GRID_W)[None, :]
    c0 = np.clip(c - NA_WIN_COLS // 2, 0, GRID_W - NA_WIN_COLS)
    col_ok = (c2 >= c0) & (c2 < c0 + NA_WIN_COLS)
    col_sel = np.eye(n_dc, dtype=np.float32)[np.clip(c2 - c + NA_WIN_COLS - 1, 0, n_dc - 1)]
    half = NA_WIN_ROWS // 2
    last_off = n_lat_rows - NA_WIN_ROWS - (n_lat_rows - NA_K_ROWS)
    last_rel = (n_lat_rows - NA_Q_ROWS) - (n_lat_rows - NA_K_ROWS)
    tables = []
    for rel, lo in ((0, 0 * a), (half, a), (last_rel, last_off + 0 * a)):
        row_ok = (j >= lo) & (j < lo + NA_WIN_ROWS)
        row_sel = np.eye(n_dr, dtype=np.float32)[np.clip(j - rel - a + NA_WIN_ROWS - 1, 0, n_dr - 1)]
        t = jnp.einsum('ajr,lhrs->lhajs', jnp.asarray(row_sel), rpb, precision=lax.Precision.HIGHEST)
        t = jnp.einsum('lhajs,cks->lhacjk', t, jnp.asarray(col_sel), precision=lax.Precision.HIGHEST)
        ok = row_ok[:, None, :, None] & col_ok[None, :, None, :]
        t = jnp.where(jnp.asarray(ok)[None, None], t, NEG_BIG)
        tables.append(t.reshape(L, H, NA_Q_ROWS * GRID_W, NA_K_ROWS * GRID_W))
    return jnp.stack(tables, axis=1)


def _merge_kernel(ya_ref, yb_ref, yc_ref, yd_ref, gates_ref, x_ref, mod_ref, wb_ref, wo_ref, bo_ref,
                  g_ref, b_ref, o_ref):
    D = x_ref.shape[-1]
    m = jnp.zeros((x_ref.shape[1], D), F32)
    for j, y_ref in enumerate((ya_ref, yb_ref, yc_ref, yd_ref)):
        gate = jax.nn.sigmoid(gates_ref[0, :, j * D:(j + 1) * D])
        m = m + gate * _mm(y_ref[0], wb_ref[j])
    out = _mm(m, wo_ref[...]) + bo_ref[...]
    g1 = mod_ref[0][:, 2 * D:3 * D]
    o_ref[0] = _ln(ALPHA * x_ref[0] + g1 * out) * g_ref[...] + b_ref[...]


def _merge(ys, gates, x, modt, wb, wo, bo, g, b, skip_tiles):
    B, T, D = x.shape
    nt = T // TILE - skip_tiles
    tok = lambda width: pl.BlockSpec((1, TILE, width), lambda bi, i: (bi, i + skip_tiles, 0))
    return pl.pallas_call(
        _merge_kernel,
        grid=(B, nt),
        in_specs=[tok(BRANCH_W)] * 4 + [tok(gates.shape[-1]), tok(D),
                  pl.BlockSpec((1, 1, 6 * D), lambda bi, i: (2 * bi + jnp.minimum(i + skip_tiles, 1), 0, 0)),
                  _const_spec(wb.shape), _const_spec(wo.shape), _const_spec(bo.shape),
                  _const_spec(g.shape), _const_spec(b.shape)],
        out_specs=pl.BlockSpec((1, TILE, D), lambda bi, i: (bi, i, 0)),
        out_shape=jax.ShapeDtypeStruct((B, nt * TILE, D), F32),
        compiler_params=_params(("parallel", "arbitrary")),
        name="merge",
    )(*ys, gates, x, modt, wb, wo, bo, g, b)


def _ffn_kernel(xp_ref, x_ref, xn_ref, mod_ref, w1_ref, b1_ref, cw_ref, cb_ref, w2_ref, b2_ref,
                g_ref, b_ref, o_ref, *, seq_first, seq_last):
    D = x_ref.shape[-1]
    d_ff = w2_ref.shape[0]
    i = pl.program_id(1)
    has_prev = functools.reduce(jnp.logical_and, [i != t for t in seq_first])
    has_next = functools.reduce(jnp.logical_and, [i != t for t in seq_last])
    mod = mod_ref[0]
    x = x_ref[0]
    n_ext = TILE + 2 * SUBLANES
    xe = jnp.concatenate([xp_ref[0], x, xn_ref[0]], axis=0)
    h = (_ln(xe) * (1.0 + mod[:, 4 * D:5 * D]) + mod[:, 3 * D:4 * D]).astype(MXU_DTYPE)
    row = lax.broadcasted_iota(jnp.int32, (n_ext, 1), 0)
    inside = ((row >= SUBLANES) | has_prev) & ((row < SUBLANES + TILE) | has_next)

    def conv(lo, hi):
        u = jnp.dot(h, w1_ref[:, lo:hi], preferred_element_type=F32) + b1_ref[:, lo:hi]
        u = jnp.where(inside, u, 0.0)
        y = (cw_ref[0:1, lo:hi] * pltpu.roll(u, 1, 0)[SUBLANES:SUBLANES + TILE]
             + cw_ref[1:2, lo:hi] * u[SUBLANES:SUBLANES + TILE]
             + cw_ref[2:3, lo:hi] * pltpu.roll(u, n_ext - 1, 0)[SUBLANES:SUBLANES + TILE])
        return y + cb_ref[:, lo:hi]

    acc = jnp.zeros((TILE, D), F32)
    for lo in range(0, d_ff, FF_CHUNK):
        hi = min(lo + FF_CHUNK, d_ff)
        val = conv(lo, hi)
        gate = conv(d_ff + lo, d_ff + hi)
        act = gate * jax.nn.sigmoid(gate) * val
        acc = acc + _mm(act, w2_ref[lo:hi, :])
    f = acc + b2_ref[...]
    g2 = mod[:, 5 * D:6 * D]
    o_ref[0] = _ln(ALPHA * x + g2 * f) * g_ref[...] + b_ref[...]


def _ffn(x, modt, w1, b1, cw, cb, w2, b2, g, b, n_ctx_tiles, latent_only):
    B, T, D = x.shape
    nt = T // TILE
    rows8 = T // SUBLANES
    per = TILE // SUBLANES
    if latent_only:
        seq_first, seq_last = (0,), (nt - 1,)
        mod_idx = lambda bi, i: (2 * bi + 1, 0, 0)
    else:
        seq_first, seq_last = (0, n_ctx_tiles), (n_ctx_tiles - 1, nt - 1)
        mod_idx = lambda bi, i: (2 * bi + jnp.minimum(i, 1), 0, 0)
    return pl.pallas_call(
        functools.partial(_ffn_kernel, seq_first=seq_first, seq_last=seq_last),
        grid=(B, nt),
        in_specs=[pl.BlockSpec((1, SUBLANES, D), lambda bi, i: (bi, jnp.maximum(i * per - 1, 0), 0)),
                  pl.BlockSpec((1, TILE, D), lambda bi, i: (bi, i, 0)),
                  pl.BlockSpec((1, SUBLANES, D), lambda bi, i: (bi, jnp.minimum((i + 1) * per, rows8 - 1), 0)),
                  pl.BlockSpec((1, 1, 6 * D), mod_idx),
                  _const_spec(w1.shape), _const_spec(b1.shape), _const_spec(cw.shape),
                  _const_spec(cb.shape), _const_spec(w2.shape), _const_spec(b2.shape),
                  _const_spec(g.shape), _const_spec(b.shape)],
        out_specs=pl.BlockSpec((1, TILE, D), lambda bi, i: (bi, i, 0)),
        out_shape=jax.ShapeDtypeStruct((B, T, D), F32),
        compiler_params=_params(("parallel", "arbitrary")),
        name="conv_ffn",
    )(x, x, x, modt, w1, b1, cw, cb, w2, b2, g, b)


def _rope_tables(n_lat, n_ctx, dim, width):
    t = jnp.arange(n_lat)
    row = (t // GRID_W).astype(F32)
    col = (t % GRID_W).astype(F32)
    n_freq = dim // 4
    inv = ROPE_BASE ** (-jnp.arange(n_freq, dtype=F32) / n_freq)
    ang = jnp.concatenate([row[:, None] * inv, col[:, None] * inv], -1)
    cos, sin = jnp.cos(ang), jnp.sin(ang)
    cos = jnp.tile(jnp.concatenate([cos, cos], -1), (1, width // dim))
    sin = jnp.tile(jnp.concatenate([-sin, sin], -1), (1, width // dim))
    cos = jnp.concatenate([jnp.ones((n_ctx, width), F32), cos], 0)
    sin = jnp.concatenate([jnp.zeros((n_ctx, width), F32), sin], 0)
    return cos, sin


def _block_diag(w):
    nb, bw, _ = w.shape
    return jnp.einsum('ncd,nm->ncmd', w, jnp.eye(nb, dtype=w.dtype)).reshape(nb * bw, nb * bw)


_GQA_HEAD_ORDER = (0, 2, 1, 3)


def _permute_heads(a, axis, order, d):
    shp = a.shape
    a = a.reshape(shp[:axis] + (len(order), d) + shp[axis + 1:])
    a = jnp.take(a, jnp.asarray(order), axis=axis)
    return a.reshape(shp)


def kernel(x, c, ctx, c_ctx, w_ada, b_ada, w_in, b_in, lru_conv_w, lru_conv_b, lru_w_a, lru_b_a, lru_w_x, lru_b_x, lru_lambda, diff_lambda, diff_subln, gqa_q_norm, gqa_k_norm, na_rpb, w_branch, w_out, b_out, ln1_g, ln1_b, ffn_w1, ffn_b1, ffn_conv_w, ffn_conv_b, ffn_w2, ffn_b2, ln2_g, ln2_b):
    B, S, D = x.shape
    C = ctx.shape[1]
    L = w_ada.shape[0]
    assert C == TILE and S % TILE == 0 and S // GRID_W >= NA_K_ROWS
    n_ctx_tiles = C // TILE

    pad = (-(B + 1)) % SUBLANES
    cvec = jnp.concatenate([c, c_ctx[None], jnp.zeros((pad, D), F32)], 0)
    mod_all = _ada(cvec, w_ada, b_ada)
    mod_lat = mod_all[:, :B]
    mod_ctx = jnp.broadcast_to(mod_all[:, B:B + 1], mod_lat.shape)
    modt = jnp.stack([mod_ctx, mod_lat], axis=2).reshape(L, 2 * B, 1, 6 * D)

    cd, sd = _rope_tables(S, C, DIFF_D, 256)
    cg, sg = _rope_tables(S, C, GQA_D, 256)
    seg64 = jnp.asarray(np.kron(np.eye(4), np.full((64, 64), 1.0 / 64)), MXU_DTYPE)

    o = _IN_OFFS
    gq_cols = _permute_heads(w_in[:, :, o[5]:o[6]], 2, _GQA_HEAD_ORDER, GQA_D)
    w_in_k = jnp.concatenate([w_in[:, :, :o[5]], gq_cols, w_in[:, :, o[6]:]], 2).astype(MXU_DTYPE)
    gq_b = _permute_heads(b_in[:, o[5]:o[6]], 1, _GQA_HEAD_ORDER, GQA_D)
    b_in_k = jnp.concatenate([b_in[:, :o[5]], gq_b, b_in[:, o[6]:]], 1)[:, None, :]
    wb_k = jnp.concatenate([w_branch[:, :2], _permute_heads(w_branch[:, 2:3], 2, _GQA_HEAD_ORDER, GQA_D),
                            w_branch[:, 3:]], 1).astype(MXU_DTYPE)
    wo_k = w_out.astype(MXU_DTYPE)
    w1_k = ffn_w1.astype(MXU_DTYPE)
    w2_k = ffn_w2.astype(MXU_DTYPE)
    qn = jnp.tile(gqa_q_norm, (1, GQA_Q_HEADS))[:, None, :]
    kn = jnp.tile(gqa_k_norm, (1, GQA_KV_HEADS))[:, None, :]
    wg = jnp.concatenate([jax.vmap(_block_diag)(lru_w_a[:, 0]), jax.vmap(_block_diag)(lru_w_x[:, 0]),
                          jax.vmap(_block_diag)(lru_w_a[:, 1]), jax.vmap(_block_diag)(lru_w_x[:, 1])],
                         -1).astype(MXU_DTYPE)
    bg = jnp.concatenate([lru_b_a[:, 0].reshape(L, -1), lru_b_x[:, 0].reshape(L, -1),
                          lru_b_a[:, 1].reshape(L, -1), lru_b_x[:, 1].reshape(L, -1)], -1)[:, None, :]
    sp = jax.nn.softplus(-lru_lambda.astype(F32))
    lp = diff_lambda.astype(F32)
    lam_delta = jnp.exp(jnp.sum(lp[:, 0] * lp[:, 1], -1)) - jnp.exp(jnp.sum(lp[:, 2] * lp[:, 3], -1))
    na_bias = _na_bias_table(na_rpb.astype(F32), S // GRID_W)

    xs = jnp.concatenate([ctx, x], axis=1)
    for l in range(L):
        last = l == L - 1
        lam_init = 0.8 - 0.6 * math.exp(-0.3 * l)
        lam = (lam_delta[l] + lam_init).reshape(1, 1)
        subln = (jnp.tile(diff_subln[l], DIFF_HEADS) * (1.0 - lam_init))[None, :]

        (z_lru, dq, dk, dv, gq, gk, gv, nq, nk, nv, gates) = _in_proj(
            xs, modt[l], w_in_k[l], b_in_k[l], cd, sd, cg, sg, seg64, qn[l], kn[l])
        y_a = _lru(z_lru, lru_conv_w[l], lru_conv_b[l][None, :], wg[l], bg[l], sp[l], n_ctx_tiles)
        y_b = _diff_attn(lam, dq, dk, dv, seg64, subln, C)
        y_c = _gqa_attn(gq, gk, gv, C)
        y_d = _na_attn(nq, nk, nv, na_bias[l], C)

        skip = n_ctx_tiles if last else 0
        x1 = _merge((y_a, y_b, y_c, y_d), gates, xs, modt[l], wb_k[l], wo_k[l], b_out[l][None, :],
                    ln1_g[l][None, :], ln1_b[l][None, :], skip)
        xs = _ffn(x1, modt[l], w1_k[l], ffn_b1[l][None, :], ffn_conv_w[l], ffn_conv_b[l][None, :],
                  w2_k[l], ffn_b2[l][None, :], ln2_g[l][None, :], ln2_b[l][None, :], n_ctx_tiles, last)
    return xs
```

```python
import functools
import math

import numpy as np
import jax
import jax.numpy as jnp
from jax import lax
from jax.experimental import pallas as pl
from jax.experimental.pallas import tpu as pltpu

F32 = jnp.float32
MXU_DTYPE = jnp.bfloat16

DEPTH = 4
GRID_W = 64
LRU_WIDTH = 256
LRU_BLOCKS = 4
LRU_C = 8.0
LRU_CONV = 4
LRU_CONV_LEFT = 2
DIFF_HEADS = 4
DIFF_D = 32
DIFF_V = 64
GQA_Q_HEADS = 4
GQA_KV_HEADS = 2
GQA_D = 64
NA_HEADS = 4
NA_D = 64
NA_WIN_ROWS = 8
NA_WIN_COLS = 16
N_BRANCH = 4
BRANCH_W = 256
FFN_CONV = 3
ROPE_BASE = 10000.0
ALPHA = (2 * DEPTH) ** 0.25
EPS = 1e-6
NEG_BIG = -1e30

TILE = 256
SUBLANES = 8
NA_Q_ROWS = TILE // GRID_W
NA_K_TILES = 3
KEY_CHUNK = 256
LOG2E = 1.4426950408889634
FF_CHUNK = 512
SCAN_ROWS = 8
V7X_VMEM_LIMIT = 56 * 1024 * 1024

_IN_WIDTHS = (256, 256, 256, 256, 256, 256, 128, 128, 256, 256, 256)
_IN_OFFS = tuple(int(v) for v in np.cumsum((0,) + _IN_WIDTHS))


def _params(sem):
    return pltpu.CompilerParams(dimension_semantics=sem, vmem_limit_bytes=V7X_VMEM_LIMIT)


def _const_spec(shape):
    nd = len(shape)
    return pl.BlockSpec(shape, lambda *_: (0,) * nd)


def _ln(x):
    mu = jnp.mean(x, axis=-1, keepdims=True)
    xc = x - mu
    var = jnp.mean(xc * xc, axis=-1, keepdims=True)
    return xc * lax.rsqrt(var + EPS)


def _mm(a, b):
    return jnp.dot(a.astype(MXU_DTYPE), b, preferred_element_type=F32)


def _mm_nt(a, b):
    return lax.dot_general(a, b, (((1,), (1,)), ((), ())), preferred_element_type=F32)


def _seg_mean(x2, seg_ref):
    hi = x2.astype(MXU_DTYPE)
    lo = (x2 - hi.astype(F32)).astype(MXU_DTYPE)
    seg = seg_ref[...]
    return (jnp.dot(hi, seg, preferred_element_type=F32)
            + jnp.dot(lo, seg, preferred_element_type=F32))


def _lane_mask(width, lo, hi):
    lane = lax.broadcasted_iota(jnp.int32, (1, width), 1)
    return (lane >= lo) & (lane < hi)


def _rope(x, cos, sin_signed, dim):
    n = x.shape[-1]
    half = dim // 2
    lane = lax.broadcasted_iota(jnp.int32, (1, n), 1)
    first = (lane & (dim - 1)) < half
    swapped = jnp.where(first, pltpu.roll(x, n - half, 1), pltpu.roll(x, half, 1))
    return x * cos + swapped * sin_signed


def _ada_kernel(c_ref, w_ref, b_ref, o_ref):
    c = c_ref[...]
    s = c * jax.nn.sigmoid(c)
    o_ref[0] = _mm(s, w_ref[0].astype(MXU_DTYPE)) + b_ref[0]


def _ada(cvec, w_ada, b_ada):
    L, D, N = w_ada.shape
    tn = 1536
    R = cvec.shape[0]
    return pl.pallas_call(
        _ada_kernel,
        grid=(L, N // tn),
        in_specs=[pl.BlockSpec((R, D), lambda l, j: (0, 0)),
                  pl.BlockSpec((1, D, tn), lambda l, j: (l, 0, j)),
                  pl.BlockSpec((1, 1, tn), lambda l, j: (l, 0, j))],
        out_specs=pl.BlockSpec((1, R, tn), lambda l, j: (l, 0, j)),
        out_shape=jax.ShapeDtypeStruct((L, R, N), F32),
        compiler_params=_params(("parallel", "parallel")),
        name="ada_mod",
    )(cvec, w_ada, b_ada.reshape(L, 1, N))


def _in_kernel(x_ref, mod_ref, w_ref, b_ref, cd_ref, sd_ref, cg_ref, sg_ref, seg_ref, qn_ref, kn_ref,
               lru_ref, dq_ref, dk_ref, dv_ref, gq_ref, gk_ref, gv_ref, nq_ref, nk_ref, nv_ref,
               gates_ref):
    D = x_ref.shape[-1]
    mod = mod_ref[0]
    h = (_ln(x_ref[0]) * (1.0 + mod[:, D:2 * D]) + mod[:, 0:D]).astype(MXU_DTYPE)

    def proj(lo, hi):
        return jnp.dot(h, w_ref[:, lo:hi], preferred_element_type=F32) + b_ref[:, lo:hi]

    o = _IN_OFFS
    lru_ref[0] = proj(o[0], o[2])

    cd, sd = cd_ref[...], sd_ref[...]
    dq_ref[0] = (_rope(proj(o[2], o[3]), cd, sd, DIFF_D) * (DIFF_D ** -0.5 * LOG2E)).astype(dq_ref.dtype)
    dk_ref[0] = _rope(proj(o[3], o[4]), cd, sd, DIFF_D).astype(dk_ref.dtype)
    dv_ref[0] = proj(o[4], o[5]).T.astype(dv_ref.dtype)

    cg, sg = cg_ref[...], sg_ref[...]
    gq = proj(o[5], o[6])
    gq = gq * lax.rsqrt(_seg_mean(gq * gq, seg_ref) + EPS) * qn_ref[...]
    gq_ref[0] = (_rope(gq, cg, sg, GQA_D) * (GQA_D ** -0.5 * LOG2E)).astype(gq_ref.dtype)
    kw = o[7] - o[6]
    gk = proj(o[6], o[7])
    gk = gk * lax.rsqrt(_seg_mean(gk * gk, seg_ref.at[0:kw, 0:kw]) + EPS) * kn_ref[...]
    gk_ref[0] = _rope(gk, cg[:, 0:kw], sg[:, 0:kw], GQA_D).astype(gk_ref.dtype)
    gv_ref[0] = proj(o[7], o[8]).T.astype(gv_ref.dtype)

    nq_ref[0] = (proj(o[8], o[9]) * (NA_D ** -0.5 * LOG2E)).astype(nq_ref.dtype)
    nk_ref[0] = proj(o[9], o[10]).astype(nk_ref.dtype)
    nv_ref[0] = proj(o[10], o[11]).T.astype(nv_ref.dtype)

    n_gate = gates_ref.shape[-1]
    for j in range(N_BRANCH):
        w = n_gate // N_BRANCH
        gates_ref[0, :, j * w:(j + 1) * w] = jax.nn.sigmoid(
            proj(o[11] + j * w, o[11] + (j + 1) * w)).astype(gates_ref.dtype)


def _in_proj(x, modt, w, b, cd, sd, cg, sg, seg, qn, kn):
    B, T, D = x.shape
    nt = T // TILE
    n_gate = w.shape[1] - _IN_OFFS[11]
    tok = lambda width: pl.BlockSpec((1, TILE, width), lambda bi, i: (bi, i, 0))
    tab = lambda width: pl.BlockSpec((TILE, width), lambda bi, i: (i, 0))
    widths = (512, 256, 256, 256, 256, 128, 128, 256, 256, 256)
    dtypes = (F32,) + (MXU_DTYPE,) * 9
    v_outs = (3, 6, 9)
    tok_t = lambda width: pl.BlockSpec((1, width, TILE), lambda bi, i: (bi, 0, i))
    out_shape = [jax.ShapeDtypeStruct((B, wd, T) if n in v_outs else (B, T, wd), dt)
                 for n, (wd, dt) in enumerate(zip(widths, dtypes))]
    out_shape.append(jax.ShapeDtypeStruct((B, T, n_gate), MXU_DTYPE))
    return pl.pallas_call(
        _in_kernel,
        grid=(B, nt),
        in_specs=[tok(D),
                  pl.BlockSpec((1, 1, 6 * D), lambda bi, i: (2 * bi + jnp.minimum(i, 1), 0, 0)),
                  _const_spec(w.shape), _const_spec(b.shape),
                  tab(256), tab(256), tab(256), tab(256),
                  _const_spec(seg.shape), _const_spec(qn.shape), _const_spec(kn.shape)],
        out_specs=[tok_t(wd) if n in v_outs else tok(wd) for n, wd in enumerate(widths)] + [tok(n_gate)],
        out_shape=out_shape,
        compiler_params=_params(("parallel", "arbitrary")),
        name="in_proj",
    )(x, modt, w, b, cd, sd, cg, sg, seg, qn, kn)


def _gelu_tanh(x):
    return 0.5 * x * (1.0 + jnp.tanh(0.7978845608028654 * (x + 0.044715 * (x * x * x))))


def _lru_kernel(z_ref, cw_ref, cb_ref, wg_ref, bg_ref, sp_ref, y_ref,
                af_ref, bf_ref, ab_ref, bb_ref, hf_ref, hb_ref, *, n_ctx_tiles):
    T = z_ref.shape[1]
    W = LRU_WIDTH
    nt = T // TILE
    last = nt - 1
    a_refs = (af_ref, ab_ref)
    b_refs = (bf_ref, bb_ref)

    def coeff_tile(c, carry):
        r0 = pl.multiple_of(c * TILE, TILE)
        seg_first = (c == 0) | (c == n_ctx_tiles)
        seg_last = (c == n_ctx_tiles - 1) | (c == last)
        p0 = pl.multiple_of(jnp.maximum(r0 - SUBLANES, 0), SUBLANES)
        n0 = pl.multiple_of(jnp.minimum(r0 + TILE, T - SUBLANES), SUBLANES)
        prev = jnp.where(seg_first, 0.0, z_ref[0, pl.ds(p0, SUBLANES), 0:W])
        nxt = jnp.where(seg_last, 0.0, z_ref[0, pl.ds(n0, SUBLANES), 0:W])
        ext = jnp.concatenate([prev, z_ref[0, pl.ds(r0, TILE), 0:W], nxt], axis=0)
        n_ext = TILE + 2 * SUBLANES
        xc = cb_ref[...]
        for k in range(LRU_CONV):
            off = k - LRU_CONV_LEFT
            sh = ext if off == 0 else pltpu.roll(ext, (-off) % n_ext, 0)
            xc = xc + cw_ref[k:k + 1, :] * sh[SUBLANES:SUBLANES + TILE]
        pre = _mm(xc, wg_ref[...]) + bg_ref[...]
        for d in range(2):
            r = jax.nn.sigmoid(pre[:, 2 * d * W:(2 * d + 1) * W])
            gate_i = jax.nn.sigmoid(pre[:, (2 * d + 1) * W:(2 * d + 2) * W])
            log_a = (-LRU_C) * r * sp_ref[d:d + 1, :]
            a2 = jnp.exp(2.0 * log_a)
            mult = jnp.sqrt((1.0 + a2) * jnp.tanh(-log_a))
            a_refs[d][pl.ds(r0, TILE), :] = jnp.exp(log_a)
            b_refs[d][pl.ds(r0, TILE), :] = mult * gate_i * xc
        return carry

    lax.fori_loop(0, nt, coeff_tile, 0)

    row = lax.broadcasted_iota(jnp.int32, (SCAN_ROWS, W), 0)
    n_chunks = T // SCAN_ROWS
    ctx_chunks = n_ctx_tiles * TILE // SCAN_ROWS

    def scan_step(j, carry):
        h_f, h_b = carry
        jf = pl.multiple_of(j * SCAN_ROWS, SCAN_ROWS)
        a = af_ref[pl.ds(jf, SCAN_ROWS), :]
        b = bf_ref[pl.ds(jf, SCAN_ROWS), :]
        for s in (1, 2, 4):
            m = row >= s
            b = jnp.where(m, a * pltpu.roll(b, s, 0) + b, b)
            a = jnp.where(m, a * pltpu.roll(a, s, 0), a)
        h = b + a * h_f
        hf_ref[pl.ds(jf, SCAN_ROWS), :] = h
        h_f = jnp.broadcast_to(h[SCAN_ROWS - 1:SCAN_ROWS, :], (SCAN_ROWS, W))

        cb = jnp.where(j < ctx_chunks, ctx_chunks - 1 - j, n_chunks - 1 + ctx_chunks - j)
        jb = pl.multiple_of(cb * SCAN_ROWS, SCAN_ROWS)
        a = ab_ref[pl.ds(jb, SCAN_ROWS), :]
        b = bb_ref[pl.ds(jb, SCAN_ROWS), :]
        for s in (1, 2, 4):
            m = row < SCAN_ROWS - s
            b = jnp.where(m, a * pltpu.roll(b, SCAN_ROWS - s, 0) + b, b)
            a = jnp.where(m, a * pltpu.roll(a, SCAN_ROWS - s, 0), a)
        h = b + a * h_b
        hb_ref[pl.ds(jb, SCAN_ROWS), :] = h
        h_b = jnp.broadcast_to(h[0:1, :], (SCAN_ROWS, W))
        return h_f, h_b

    zero = jnp.zeros((SCAN_ROWS, W), F32)
    lax.fori_loop(0, n_chunks, scan_step, (zero, zero))

    def out_tile(c, carry):
        r0 = pl.multiple_of(c * TILE, TILE)
        g = z_ref[0, pl.ds(r0, TILE), W:2 * W]
        y_ref[0, pl.ds(r0, TILE), :] = (hf_ref[pl.ds(r0, TILE), :] + hb_ref[pl.ds(r0, TILE), :]) * _gelu_tanh(g)
        return carry

    lax.fori_loop(0, nt, out_tile, 0)


def _lru(z, cw, cb, wg, bg, sp, n_ctx_tiles):
    B, T, _ = z.shape
    W = LRU_WIDTH
    return pl.pallas_call(
        functools.partial(_lru_kernel, n_ctx_tiles=n_ctx_tiles),
        grid=(B,),
        in_specs=[pl.BlockSpec((1, T, 2 * W), lambda bi: (bi, 0, 0)),
                  _const_spec(cw.shape), _const_spec(cb.shape), _const_spec(wg.shape),
                  _const_spec(bg.shape), _const_spec(sp.shape)],
        out_specs=pl.BlockSpec((1, T, W), lambda bi: (bi, 0, 0)),
        out_shape=jax.ShapeDtypeStruct((B, T, W), F32),
        scratch_shapes=[pltpu.VMEM((T, W), F32) for _ in range(6)],
        compiler_params=_params(("parallel",)),
        name="rglru",
    )(z, cw, cb, wg, bg, sp)


def _head_q(q, lo, hi):
    return jnp.where(_lane_mask(q.shape[-1], lo, hi), q.astype(F32), 0.0).astype(MXU_DTYPE)


def _fold_slabs(x, op):
    slabs = [x[r:r + SUBLANES] for r in range(0, x.shape[0], SUBLANES)]
    while len(slabs) > 1:
        nxt = [op(slabs[i], slabs[i + 1]) for i in range(0, len(slabs) - 1, 2)]
        if len(slabs) % 2:
            nxt.append(slabs[-1])
        slabs = nxt
    return slabs[0]


def _attend_maps(maps, s_refs):
    n_maps = len(maps)
    tq = maps[0][0].shape[0]
    results = []
    m_prev = None
    for n in range(n_maps + 1):
        macc, lacc, o_t = None, jnp.zeros((SUBLANES, tq), F32), None
        n_chunks = len(maps[min(n, n_maps - 1)][1])
        off = 0
        for c in range(n_chunks):
            if n < n_maps:
                qm, chunks = maps[n]
                k_fn, bias_fn, _ = chunks[c]
                s = _mm_nt(k_fn(), qm)
                if bias_fn is not None:
                    s = s + bias_fn()
                rows = s.shape[0]
                s_refs[n % 2][off:off + rows, :] = s
                mx = _fold_slabs(s, jnp.maximum)
                macc = mx if macc is None else jnp.maximum(macc, mx)
            if n >= 1:
                _, chunks = maps[n - 1]
                vt = chunks[c][2]()
                rows = vt.shape[1]
                e = jnp.exp2(s_refs[(n - 1) % 2][off:off + rows, :] - m_prev)
                lacc = lacc + _fold_slabs(e, jnp.add)
                o = jnp.dot(vt, e.astype(MXU_DTYPE), preferred_element_type=F32)
                o_t = o if o_t is None else o_t + o
            off += rows
        if n >= 1:
            results.append((o_t, jnp.sum(lacc, axis=0, keepdims=True)))
        if n < n_maps:
            m_prev = jnp.max(macc, axis=0, keepdims=True)
    return results


def _key_chunks(k_ref, vt_ref, v_rows, nk):
    return [(lambda r0=r0, n=min(KEY_CHUNK, nk - r0): k_ref[0, r0:r0 + n, :], None,
             lambda r0=r0, n=min(KEY_CHUNK, nk - r0): vt_ref[0, v_rows, r0:r0 + n])
            for r0 in range(0, nk, KEY_CHUNK)]


def _diff_kernel(lam_ref, q_ref, k_ref, vt_ref, g_ref, o_ref, s0_ref, s1_ref, *, n_ctx):
    def attend(nk):
        q = q_ref[0]
        lam = lam_ref[...]
        maps = []
        for h in range(DIFF_HEADS):
            chunks = _key_chunks(k_ref, vt_ref, slice(h * DIFF_V, (h + 1) * DIFF_V), nk)
            for mth in range(2):
                c0 = (2 * h + mth) * DIFF_D
                maps.append((_head_q(q, c0, c0 + DIFF_D), chunks))
        res = _attend_maps(maps, (s0_ref, s1_ref))
        heads = []
        for h in range(DIFF_HEADS):
            (o1, l1), (o2, l2) = res[2 * h], res[2 * h + 1]
            o = o1 * (1.0 / l1) - o2 * (lam / l2)
            ms = jnp.mean(o * o, axis=0, keepdims=True)
            heads.append(o * lax.rsqrt(ms + EPS))
        o_ref[0] = jnp.concatenate(heads, axis=0).T * g_ref[...]

    i = pl.program_id(1)

    @pl.when(i == 0)
    def _():
        attend(n_ctx)

    @pl.when(i > 0)
    def _():
        attend(k_ref.shape[1])


def _diff_attn(lam, q, k, vt, g, n_ctx):
    B, T, W = q.shape
    return pl.pallas_call(
        functools.partial(_diff_kernel, n_ctx=n_ctx),
        grid=(B, T // TILE),
        in_specs=[_const_spec(lam.shape),
                  pl.BlockSpec((1, TILE, W), lambda bi, i: (bi, i, 0)),
                  pl.BlockSpec((1, T, W), lambda bi, i: (bi, 0, 0)),
                  pl.BlockSpec((1, W, T), lambda bi, i: (bi, 0, 0)),
                  _const_spec(g.shape)],
        out_specs=pl.BlockSpec((1, TILE, W), lambda bi, i: (bi, i, 0)),
        out_shape=jax.ShapeDtypeStruct((B, T, W), F32),
        scratch_shapes=[pltpu.VMEM((T, TILE), F32), pltpu.VMEM((T, TILE), F32)],
        compiler_params=_params(("parallel", "arbitrary")),
        name="diff_attn",
    )(lam, q, k, vt, g)


def _gqa_kernel(q_ref, k_ref, vt_ref, o_ref, s0_ref, s1_ref, *, n_ctx):
    def attend(nk):
        kvw = k_ref.shape[-1]
        group = GQA_Q_HEADS // GQA_KV_HEADS
        maps = []
        for h in range(GQA_Q_HEADS):
            j, half = h // group, h % group
            q = q_ref[0, :, half * kvw:(half + 1) * kvw]
            chunks = _key_chunks(k_ref, vt_ref, slice(j * GQA_D, (j + 1) * GQA_D), nk)
            maps.append((_head_q(q, j * GQA_D, (j + 1) * GQA_D), chunks))
        res = _attend_maps(maps, (s0_ref, s1_ref))
        o_ref[0] = jnp.concatenate([o * (1.0 / l) for o, l in res], axis=0).T

    i = pl.program_id(1)

    @pl.when(i == 0)
    def _():
        attend(n_ctx)

    @pl.when(i > 0)
    def _():
        attend(k_ref.shape[1])


def _gqa_attn(q, k, vt, n_ctx):
    B, T, W = q.shape
    KW = k.shape[-1]
    return pl.pallas_call(
        functools.partial(_gqa_kernel, n_ctx=n_ctx),
        grid=(B, T // TILE),
        in_specs=[pl.BlockSpec((1, TILE, W), lambda bi, i: (bi, i, 0)),
                  pl.BlockSpec((1, T, KW), lambda bi, i: (bi, 0, 0)),
                  pl.BlockSpec((1, KW, T), lambda bi, i: (bi, 0, 0))],
        out_specs=pl.BlockSpec((1, TILE, W), lambda bi, i: (bi, i, 0)),
        out_shape=jax.ShapeDtypeStruct((B, T, W), F32),
        scratch_shapes=[pltpu.VMEM((T, TILE), F32), pltpu.VMEM((T, TILE), F32)],
        compiler_params=_params(("parallel", "arbitrary")),
        name="gqa_attn",
    )(q, k, vt)


def _na_kernel(q_ref, kc_ref, k0_ref, k1_ref, k2_ref, vc_ref, v0_ref, v1_ref, v2_ref, bias_ref, o_ref,
               s0_ref, s1_ref):
    i = pl.program_id(1)

    def run(windowed):
        q = q_ref[0]
        maps = []
        for h in range(NA_HEADS):
            rows = slice(h * NA_D, (h + 1) * NA_D)
            chunks = [(lambda: kc_ref[0], None, lambda rows=rows: vc_ref[0, rows, :])]
            if windowed:
                for t, (kw_ref, vw_ref) in enumerate(((k0_ref, v0_ref), (k1_ref, v1_ref), (k2_ref, v2_ref))):
                    chunks.append((lambda kw_ref=kw_ref: kw_ref[0],
                                   lambda h=h, t=t: bias_ref[0, h, t * TILE:(t + 1) * TILE, :],
                                   lambda rows=rows, vw_ref=vw_ref: vw_ref[0, rows, :]))
            maps.append((_head_q(q, h * NA_D, (h + 1) * NA_D), chunks))
        res = _attend_maps(maps, (s0_ref, s1_ref))
        o_ref[0] = jnp.concatenate([o * (1.0 / l) for o, l in res], axis=0).T

    @pl.when(i == 0)
    def _():
        run(False)

    @pl.when(i > 0)
    def _():
        run(True)


def _na_attn(q, k, vt, bias, n_ctx_tiles):
    B, T, W = q.shape
    nt = T // TILE
    n_lat_tiles = nt - n_ctx_tiles

    def key_tile0(i):
        first = (i - n_ctx_tiles) - (NA_WIN_ROWS // 2) // NA_Q_ROWS
        return n_ctx_tiles + jnp.clip(first, 0, n_lat_tiles - NA_K_TILES)

    def bias_idx(bi, i):
        kind = jnp.where(i <= n_ctx_tiles, 0, jnp.where(i == nt - 1, 2, 1))
        return (kind, 0, 0, 0)

    k_spec = lambda t: pl.BlockSpec((1, TILE, W), lambda bi, i: (bi, key_tile0(i) + t, 0))
    v_spec = lambda t: pl.BlockSpec((1, W, TILE), lambda bi, i: (bi, 0, key_tile0(i) + t))
    return pl.pallas_call(
        _na_kernel,
        grid=(B, nt),
        in_specs=[pl.BlockSpec((1, TILE, W), lambda bi, i: (bi, i, 0)),
                  pl.BlockSpec((1, TILE, W), lambda bi, i: (bi, 0, 0)), k_spec(0), k_spec(1), k_spec(2),
                  pl.BlockSpec((1, W, TILE), lambda bi, i: (bi, 0, 0)), v_spec(0), v_spec(1), v_spec(2),
                  pl.BlockSpec((1,) + bias.shape[1:], bias_idx)],
        out_specs=pl.BlockSpec((1, TILE, W), lambda bi, i: (bi, i, 0)),
        out_shape=jax.ShapeDtypeStruct((B, T, W), F32),
        scratch_shapes=[pltpu.VMEM(((NA_K_TILES + 1) * TILE, TILE), F32) for _ in range(2)],
        compiler_params=_params(("parallel", "arbitrary")),
        name="na_attn",
    )(q, k, k, k, k, vt, vt, vt, vt, bias)


def _na_bias_table(rpb, n_lat_rows):
    L, H, n_dr, n_dc = rpb.shape
    k_rows = NA_K_TILES * NA_Q_ROWS
    a = np.arange(NA_Q_ROWS)[:, None]
    j = np.arange(k_rows)[None, :]
    c = np.arange(GRID_W)[:, None]
    c2 = np.arange(GRID_W)[None, :]
    c0 = np.clip(c - NA_WIN_COLS // 2, 0, GRID_W - NA_WIN_COLS)
    col_ok = (c2 >= c0) & (c2 < c0 + NA_WIN_COLS)
    col_sel = np.eye(n_dc, dtype=np.float32)[np.clip(c2 - c + NA_WIN_COLS - 1, 0, n_dc - 1)]
    half = NA_WIN_ROWS // 2
    last_off = n_lat_rows - NA_WIN_ROWS - (n_lat_rows - k_rows)
    last_rel = (n_lat_rows - NA_Q_ROWS) - (n_lat_rows - k_rows)
    tables = []
    for rel, lo in ((0, 0 * a), (half, a), (last_rel, last_off + 0 * a)):
        row_ok = (j >= lo) & (j < lo + NA_WIN_ROWS)
        row_sel = np.eye(n_dr, dtype=np.float32)[np.clip(j - rel - a + NA_WIN_ROWS - 1, 0, n_dr - 1)]
        t = jnp.einsum('ajr,lhrs->lhajs', jnp.asarray(row_sel), rpb, precision=lax.Precision.HIGHEST)
        t = jnp.einsum('lhajs,cks->lhjkac', t, jnp.asarray(col_sel), precision=lax.Precision.HIGHEST)
        ok = row_ok.T[:, None, :, None] & col_ok.T[None, :, None, :]
        t = jnp.where(jnp.asarray(ok)[None, None], t * LOG2E, NEG_BIG)
        tables.append(t.reshape(L, H, k_rows * GRID_W, NA_Q_ROWS * GRID_W))
    return jnp.stack(tables, axis=1)


def _merge_kernel(ya_ref, yb_ref, yc_ref, yd_ref, gates_ref, x_ref, mod_ref, wb_ref, wo_ref, bo_ref,
                  g_ref, b_ref, o_ref):
    D = x_ref.shape[-1]
    m = jnp.zeros((x_ref.shape[1], D), F32)
    for j, y_ref in enumerate((ya_ref, yb_ref, yc_ref, yd_ref)):
        gate = gates_ref[0, :, j * D:(j + 1) * D].astype(F32)
        m = m + gate * _mm(y_ref[0], wb_ref[j])
    out = _mm(m, wo_ref[...]) + bo_ref[...]
    g1 = mod_ref[0][:, 2 * D:3 * D]
    o_ref[0] = _ln(ALPHA * x_ref[0] + g1 * out) * g_ref[...] + b_ref[...]


def _merge(ys, gates, x, modt, wb, wo, bo, g, b, skip_tiles):
    B, T, D = x.shape
    nt = T // TILE - skip_tiles
    tok = lambda width: pl.BlockSpec((1, TILE, width), lambda bi, i: (bi, i + skip_tiles, 0))
    return pl.pallas_call(
        _merge_kernel,
        grid=(B, nt),
        in_specs=[tok(BRANCH_W)] * 4 + [tok(gates.shape[-1]), tok(D),
                  pl.BlockSpec((1, 1, 6 * D), lambda bi, i: (2 * bi + jnp.minimum(i + skip_tiles, 1), 0, 0)),
                  _const_spec(wb.shape), _const_spec(wo.shape), _const_spec(bo.shape),
                  _const_spec(g.shape), _const_spec(b.shape)],
        out_specs=pl.BlockSpec((1, TILE, D), lambda bi, i: (bi, i, 0)),
        out_shape=jax.ShapeDtypeStruct((B, nt * TILE, D), F32),
        compiler_params=_params(("parallel", "arbitrary")),
        name="merge",
    )(*ys, gates, x, modt, wb, wo, bo, g, b)


def _ffn_kernel(xp_ref, x_ref, xn_ref, mod_ref, w1_ref, b1_ref, cw_ref, cb_ref, w2_ref, b2_ref,
                g_ref, b_ref, o_ref, *, seq_first, seq_last):
    D = x_ref.shape[-1]
    d_ff = w2_ref.shape[0]
    i = pl.program_id(1)
    has_prev = functools.reduce(jnp.logical_and, [i != t for t in seq_first])
    has_next = functools.reduce(jnp.logical_and, [i != t for t in seq_last])
    mod = mod_ref[0]
    x = x_ref[0]
    n_ext = TILE + 2 * SUBLANES
    xe = jnp.concatenate([xp_ref[0], x, xn_ref[0]], axis=0)
    h = (_ln(xe) * (1.0 + mod[:, 4 * D:5 * D]) + mod[:, 3 * D:4 * D]).astype(MXU_DTYPE)

    def up(lo, hi):
        u = jnp.dot(h, w1_ref[:, lo:hi], preferred_element_type=F32) + b1_ref[:, lo:hi]
        return jnp.concatenate([jnp.where(has_prev, u[0:SUBLANES], 0.0), u[SUBLANES:SUBLANES + TILE],
                                jnp.where(has_next, u[SUBLANES + TILE:], 0.0)], axis=0)

    def conv(u, lo, hi):
        y = (cw_ref[0:1, lo:hi] * pltpu.roll(u, 1, 0)[SUBLANES:SUBLANES + TILE]
             + cw_ref[1:2, lo:hi] * u[SUBLANES:SUBLANES + TILE]
             + cw_ref[2:3, lo:hi] * pltpu.roll(u, n_ext - 1, 0)[SUBLANES:SUBLANES + TILE])
        return y + cb_ref[:, lo:hi]

    chunks = [(lo, min(lo + FF_CHUNK, d_ff)) for lo in range(0, d_ff, FF_CHUNK)]
    acc = jnp.zeros((TILE, D), F32)
    pending = None
    for c in range(len(chunks) + 1):
        nxt = None
        if c < len(chunks):
            lo, hi = chunks[c]
            nxt = (up(lo, hi), up(d_ff + lo, d_ff + hi))
        if pending is not None:
            lo, hi = chunks[c - 1]
            val = conv(pending[0], lo, hi)
            gate = conv(pending[1], d_ff + lo, d_ff + hi)
            acc = acc + _mm(gate * jax.nn.sigmoid(gate) * val, w2_ref[lo:hi, :])
        pending = nxt
    f = acc + b2_ref[...]
    g2 = mod[:, 5 * D:6 * D]
    o_ref[0] = _ln(ALPHA * x + g2 * f) * g_ref[...] + b_ref[...]


def _ffn(x, modt, w1, b1, cw, cb, w2, b2, g, b, n_ctx_tiles, latent_only):
    B, T, D = x.shape
    nt = T // TILE
    rows8 = T // SUBLANES
    per = TILE // SUBLANES
    if latent_only:
        seq_first, seq_last = (0,), (nt - 1,)
        mod_idx = lambda bi, i: (2 * bi + 1, 0, 0)
    else:
        seq_first, seq_last = (0, n_ctx_tiles), (n_ctx_tiles - 1, nt - 1)
        mod_idx = lambda bi, i: (2 * bi + jnp.minimum(i, 1), 0, 0)
    return pl.pallas_call(
        functools.partial(_ffn_kernel, seq_first=seq_first, seq_last=seq_last),
        grid=(B, nt),
        in_specs=[pl.BlockSpec((1, SUBLANES, D), lambda bi, i: (bi, jnp.maximum(i * per - 1, 0), 0)),
                  pl.BlockSpec((1, TILE, D), lambda bi, i: (bi, i, 0)),
                  pl.BlockSpec((1, SUBLANES, D), lambda bi, i: (bi, jnp.minimum((i + 1) * per, rows8 - 1), 0)),
                  pl.BlockSpec((1, 1, 6 * D), mod_idx),
                  _const_spec(w1.shape), _const_spec(b1.shape), _const_spec(cw.shape),
                  _const_spec(cb.shape), _const_spec(w2.shape), _const_spec(b2.shape),
                  _const_spec(g.shape), _const_spec(b.shape)],
        out_specs=pl.BlockSpec((1, TILE, D), lambda bi, i: (bi, i, 0)),
        out_shape=jax.ShapeDtypeStruct((B, T, D), F32),
        compiler_params=_params(("parallel", "arbitrary")),
        name="conv_ffn",
    )(x, x, x, modt, w1, b1, cw, cb, w2, b2, g, b)


def _rope_tables(n_lat, n_ctx, dim, width):
    t = jnp.arange(n_lat)
    row = (t // GRID_W).astype(F32)
    col = (t % GRID_W).astype(F32)
    n_freq = dim // 4
    inv = ROPE_BASE ** (-jnp.arange(n_freq, dtype=F32) / n_freq)
    ang = jnp.concatenate([row[:, None] * inv, col[:, None] * inv], -1)
    cos, sin = jnp.cos(ang), jnp.sin(ang)
    cos = jnp.tile(jnp.concatenate([cos, cos], -1), (1, width // dim))
    sin = jnp.tile(jnp.concatenate([-sin, sin], -1), (1, width // dim))
    cos = jnp.concatenate([jnp.ones((n_ctx, width), F32), cos], 0)
    sin = jnp.concatenate([jnp.zeros((n_ctx, width), F32), sin], 0)
    return cos, sin


def _block_diag(w):
    nb, bw, _ = w.shape
    return jnp.einsum('ncd,nm->ncmd', w, jnp.eye(nb, dtype=w.dtype)).reshape(nb * bw, nb * bw)


_GQA_HEAD_ORDER = (0, 2, 1, 3)


def _permute_heads(a, axis, order, d):
    shp = a.shape
    a = a.reshape(shp[:axis] + (len(order), d) + shp[axis + 1:])
    a = jnp.take(a, jnp.asarray(order), axis=axis)
    return a.reshape(shp)


def kernel(x, c, ctx, c_ctx, w_ada, b_ada, w_in, b_in, lru_conv_w, lru_conv_b, lru_w_a, lru_b_a, lru_w_x, lru_b_x, lru_lambda, diff_lambda, diff_subln, gqa_q_norm, gqa_k_norm, na_rpb, w_branch, w_out, b_out, ln1_g, ln1_b, ffn_w1, ffn_b1, ffn_conv_w, ffn_conv_b, ffn_w2, ffn_b2, ln2_g, ln2_b):
    B, S, D = x.shape
    C = ctx.shape[1]
    L = w_ada.shape[0]
    assert C == TILE and S % TILE == 0 and S // TILE >= NA_K_TILES
    n_ctx_tiles = C // TILE

    pad = (-(B + 1)) % SUBLANES
    cvec = jnp.concatenate([c, c_ctx[None], jnp.zeros((pad, D), F32)], 0)
    mod_all = _ada(cvec, w_ada, b_ada)
    mod_lat = mod_all[:, :B]
    mod_ctx = jnp.broadcast_to(mod_all[:, B:B + 1], mod_lat.shape)
    modt = jnp.stack([mod_ctx, mod_lat], axis=2).reshape(L, 2 * B, 1, 6 * D)

    cd, sd = _rope_tables(S, C, DIFF_D, 256)
    cg, sg = _rope_tables(S, C, GQA_D, 256)
    seg64 = jnp.asarray(np.kron(np.eye(4), np.full((64, 64), 1.0 / 64)), MXU_DTYPE)

    o = _IN_OFFS
    gq_cols = _permute_heads(w_in[:, :, o[5]:o[6]], 2, _GQA_HEAD_ORDER, GQA_D)
    w_in_k = jnp.concatenate([w_in[:, :, :o[5]], gq_cols, w_in[:, :, o[6]:]], 2).astype(MXU_DTYPE)
    gq_b = _permute_heads(b_in[:, o[5]:o[6]], 1, _GQA_HEAD_ORDER, GQA_D)
    b_in_k = jnp.concatenate([b_in[:, :o[5]], gq_b, b_in[:, o[6]:]], 1)[:, None, :]
    wb_k = w_branch.astype(MXU_DTYPE)
    wo_k = w_out.astype(MXU_DTYPE)
    w1_k = ffn_w1.astype(MXU_DTYPE)
    w2_k = ffn_w2.astype(MXU_DTYPE)
    qn = jnp.tile(gqa_q_norm, (1, GQA_Q_HEADS))[:, None, :]
    kn = jnp.tile(gqa_k_norm, (1, GQA_KV_HEADS))[:, None, :]
    wg = jnp.concatenate([jax.vmap(_block_diag)(lru_w_a[:, 0]), jax.vmap(_block_diag)(lru_w_x[:, 0]),
                          jax.vmap(_block_diag)(lru_w_a[:, 1]), jax.vmap(_block_diag)(lru_w_x[:, 1])],
                         -1).astype(MXU_DTYPE)
    bg = jnp.concatenate([lru_b_a[:, 0].reshape(L, -1), lru_b_x[:, 0].reshape(L, -1),
                          lru_b_a[:, 1].reshape(L, -1), lru_b_x[:, 1].reshape(L, -1)], -1)[:, None, :]
    sp = jax.nn.softplus(-lru_lambda.astype(F32))
    lp = diff_lambda.astype(F32)
    lam_delta = jnp.exp(jnp.sum(lp[:, 0] * lp[:, 1], -1)) - jnp.exp(jnp.sum(lp[:, 2] * lp[:, 3], -1))
    na_bias = _na_bias_table(na_rpb.astype(F32), S // GRID_W)

    xs = jnp.concatenate([ctx, x], axis=1)
    for l in range(L):
        last = l == L - 1
        lam_init = 0.8 - 0.6 * math.exp(-0.3 * l)
        lam = (lam_delta[l] + lam_init).reshape(1, 1)
        subln = (jnp.tile(diff_subln[l], DIFF_HEADS) * (1.0 - lam_init))[None, :]

        (z_lru, dq, dk, dv, gq, gk, gv, nq, nk, nv, gates) = _in_proj(
            xs, modt[l], w_in_k[l], b_in_k[l], cd, sd, cg, sg, seg64, qn[l], kn[l])
        y_a = _lru(z_lru, lru_conv_w[l], lru_conv_b[l][None, :], wg[l], bg[l], sp[l], n_ctx_tiles)
        y_b = _diff_attn(lam, dq, dk, dv, subln, C)
        y_c = _gqa_attn(gq, gk, gv, C)
        y_d = _na_attn(nq, nk, nv, na_bias[l], n_ctx_tiles)

        skip = n_ctx_tiles if last else 0
        x1 = _merge((y_a, y_b, y_c, y_d), gates, xs, modt[l], wb_k[l], wo_k[l], b_out[l][None, :],
                    ln1_g[l][None, :], ln1_b[l][None, :], skip)
        xs = _ffn(x1, modt[l], w1_k[l], ffn_b1[l][None, :], ffn_conv_w[l], ffn_conv_b[l][None, :],
                  w2_k[l], ffn_b2[l][None, :], ln2_g[l][None, :], ln2_b[l][None, :], n_ctx_tiles, last)
    return xs
```

```python
import functools
import math

import numpy as np
import jax
import jax.numpy as jnp
from jax import lax
from jax.experimental import pallas as pl
from jax.experimental.pallas import tpu as pltpu

F32 = jnp.float32
MXU_DTYPE = jnp.bfloat16

DEPTH = 4
GRID_W = 64
LRU_WIDTH = 256
LRU_BLOCKS = 4
LRU_C = 8.0
LRU_CONV = 4
LRU_CONV_LEFT = 2
DIFF_HEADS = 4
DIFF_D = 32
DIFF_V = 64
GQA_Q_HEADS = 4
GQA_KV_HEADS = 2
GQA_D = 64
NA_HEADS = 4
NA_D = 64
NA_WIN_ROWS = 8
NA_WIN_COLS = 16
N_BRANCH = 4
BRANCH_W = 256
FFN_CONV = 3
ROPE_BASE = 10000.0
ALPHA = (2 * DEPTH) ** 0.25
EPS = 1e-6
NEG_BIG = -1e30

TILE = 256
SUBLANES = 8
NA_Q_ROWS = TILE // GRID_W
NA_K_TILES = 3
KEY_CHUNK = 256
DENOM_ROWS = 16
LOG2E = 1.4426950408889634
FF_CHUNK = 512
SCAN_ROWS = 8
V7X_VMEM_LIMIT = 56 * 1024 * 1024

_IN_WIDTHS = (256, 256, 256, 256, 256, 256, 128, 128, 256, 256, 256)
_IN_OFFS = tuple(int(v) for v in np.cumsum((0,) + _IN_WIDTHS))


def _params(sem):
    return pltpu.CompilerParams(dimension_semantics=sem, vmem_limit_bytes=V7X_VMEM_LIMIT)


def _const_spec(shape):
    nd = len(shape)
    return pl.BlockSpec(shape, lambda *_: (0,) * nd)


def _layer_spec(shape, l):
    nd = len(shape)
    return pl.BlockSpec((None,) + tuple(shape[1:]), lambda *_: (l,) + (0,) * (nd - 1))


def _ln(x):
    mu = jnp.mean(x, axis=-1, keepdims=True)
    xc = x - mu
    var = jnp.mean(xc * xc, axis=-1, keepdims=True)
    return xc * lax.rsqrt(var + EPS)


def _mm(a, b):
    return jnp.dot(a.astype(MXU_DTYPE), b, preferred_element_type=F32)


def _mm_nt(a, b):
    return lax.dot_general(a, b, (((1,), (1,)), ((), ())), preferred_element_type=F32)


def _seg_mean(x2, seg_ref):
    hi = x2.astype(MXU_DTYPE)
    lo = (x2 - hi.astype(F32)).astype(MXU_DTYPE)
    seg = seg_ref[...]
    return (jnp.dot(hi, seg, preferred_element_type=F32)
            + jnp.dot(lo, seg, preferred_element_type=F32))


def _lane_mask(width, lo, hi):
    lane = lax.broadcasted_iota(jnp.int32, (1, width), 1)
    return (lane >= lo) & (lane < hi)


def _rope(x, cos, sin_signed, dim):
    n = x.shape[-1]
    half = dim // 2
    lane = lax.broadcasted_iota(jnp.int32, (1, n), 1)
    first = (lane & (dim - 1)) < half
    swapped = jnp.where(first, pltpu.roll(x, n - half, 1), pltpu.roll(x, half, 1))
    return x * cos + swapped * sin_signed


def _ada_kernel(c_ref, w_ref, b_ref, o_ref):
    c = c_ref[...]
    s = c * jax.nn.sigmoid(c)
    o_ref[0] = _mm(s, w_ref[0].astype(MXU_DTYPE)) + b_ref[0]


def _ada(cvec, w_ada, b_ada):
    L, D, N = w_ada.shape
    tn = 1536
    R = cvec.shape[0]
    return pl.pallas_call(
        _ada_kernel,
        grid=(L, N // tn),
        in_specs=[pl.BlockSpec((R, D), lambda l, j: (0, 0)),
                  pl.BlockSpec((1, D, tn), lambda l, j: (l, 0, j)),
                  pl.BlockSpec((1, 1, tn), lambda l, j: (l, 0, j))],
        out_specs=pl.BlockSpec((1, R, tn), lambda l, j: (l, 0, j)),
        out_shape=jax.ShapeDtypeStruct((L, R, N), F32),
        compiler_params=_params(("parallel", "parallel")),
        name="ada_mod",
    )(cvec, w_ada, b_ada.reshape(L, 1, N))


def _in_kernel(x_ref, mod_ref, w_ref, b_ref, cd_ref, sd_ref, cg_ref, sg_ref, seg_ref, qn_ref, kn_ref,
               lru_ref, dq_ref, dk_ref, dv_ref, gq_ref, gk_ref, gv_ref, nq_ref, nk_ref, nv_ref,
               gates_ref):
    D = x_ref.shape[-1]
    mod = mod_ref[0]
    h = (_ln(x_ref[0]) * (1.0 + mod[:, D:2 * D]) + mod[:, 0:D]).astype(MXU_DTYPE)

    def proj(lo, hi):
        return jnp.dot(h, w_ref[:, lo:hi], preferred_element_type=F32) + b_ref[:, lo:hi]

    o = _IN_OFFS
    lru_ref[0] = proj(o[0], o[2])

    cd, sd = cd_ref[...], sd_ref[...]
    dq_ref[0] = (_rope(proj(o[2], o[3]), cd, sd, DIFF_D) * (DIFF_D ** -0.5 * LOG2E)).astype(dq_ref.dtype)
    dk_ref[0] = _rope(proj(o[3], o[4]), cd, sd, DIFF_D).astype(dk_ref.dtype)
    dv_ref[0] = proj(o[4], o[5]).T.astype(dv_ref.dtype)

    cg, sg = cg_ref[...], sg_ref[...]
    gq = proj(o[5], o[6])
    gq = gq * lax.rsqrt(_seg_mean(gq * gq, seg_ref) + EPS) * qn_ref[...]
    gq_ref[0] = (_rope(gq, cg, sg, GQA_D) * (GQA_D ** -0.5 * LOG2E)).astype(gq_ref.dtype)
    kw = o[7] - o[6]
    gk = proj(o[6], o[7])
    gk = gk * lax.rsqrt(_seg_mean(gk * gk, seg_ref.at[0:kw, 0:kw]) + EPS) * kn_ref[...]
    gk_ref[0] = _rope(gk, cg[:, 0:kw], sg[:, 0:kw], GQA_D).astype(gk_ref.dtype)
    gv_ref[0] = proj(o[7], o[8]).T.astype(gv_ref.dtype)

    nq_ref[0] = (proj(o[8], o[9]) * (NA_D ** -0.5 * LOG2E)).astype(nq_ref.dtype)
    nk_ref[0] = proj(o[9], o[10]).astype(nk_ref.dtype)
    nv_ref[0] = proj(o[10], o[11]).T.astype(nv_ref.dtype)

    n_gate = gates_ref.shape[-1]
    for j in range(N_BRANCH):
        w = n_gate // N_BRANCH
        gates_ref[0, :, j * w:(j + 1) * w] = jax.nn.sigmoid(
            proj(o[11] + j * w, o[11] + (j + 1) * w)).astype(gates_ref.dtype)


def _in_proj(x, modt, w, b, cd, sd, cg, sg, seg, qn, kn, l):
    B, T, D = x.shape
    nt = T // TILE
    n_gate = w.shape[-1] - _IN_OFFS[11]
    tok = lambda width: pl.BlockSpec((1, TILE, width), lambda bi, i: (bi, i, 0))
    tab = lambda width: pl.BlockSpec((TILE, width), lambda bi, i: (i, 0))
    widths = (512, 256, 256, 256, 256, 128, 128, 256, 256, 256)
    dtypes = (F32,) + (MXU_DTYPE,) * 9
    v_outs = (3, 6, 9)
    tok_t = lambda width: pl.BlockSpec((1, width, TILE), lambda bi, i: (bi, 0, i))
    out_shape = [jax.ShapeDtypeStruct((B, wd, T) if n in v_outs else (B, T, wd), dt)
                 for n, (wd, dt) in enumerate(zip(widths, dtypes))]
    out_shape.append(jax.ShapeDtypeStruct((B, T, n_gate), MXU_DTYPE))
    return pl.pallas_call(
        _in_kernel,
        grid=(B, nt),
        in_specs=[tok(D),
                  pl.BlockSpec((1, 1, 6 * D), lambda bi, i: (2 * bi + jnp.minimum(i, 1), 0, 0)),
                  _layer_spec(w.shape, l), _layer_spec(b.shape, l),
                  tab(256), tab(256), tab(256), tab(256),
                  _const_spec(seg.shape), _const_spec(qn.shape), _const_spec(kn.shape)],
        out_specs=[tok_t(wd) if n in v_outs else tok(wd) for n, wd in enumerate(widths)] + [tok(n_gate)],
        out_shape=out_shape,
        compiler_params=_params(("parallel", "arbitrary")),
        name="in_proj",
    )(x, modt, w, b, cd, sd, cg, sg, seg, qn, kn)


def _gelu_tanh(x):
    return 0.5 * x * (1.0 + jnp.tanh(0.7978845608028654 * (x + 0.044715 * (x * x * x))))


def _lru_kernel(z_ref, cw_ref, cb_ref, wg_ref, bg_ref, sp_ref, y_ref,
                af_ref, bf_ref, ab_ref, bb_ref, hf_ref, hb_ref, *, n_ctx_tiles):
    T = z_ref.shape[1]
    W = LRU_WIDTH
    nt = T // TILE
    last = nt - 1
    a_refs = (af_ref, ab_ref)
    b_refs = (bf_ref, bb_ref)

    def coeff_tile(c, carry):
        r0 = pl.multiple_of(c * TILE, TILE)
        seg_first = (c == 0) | (c == n_ctx_tiles)
        seg_last = (c == n_ctx_tiles - 1) | (c == last)
        p0 = pl.multiple_of(jnp.maximum(r0 - SUBLANES, 0), SUBLANES)
        n0 = pl.multiple_of(jnp.minimum(r0 + TILE, T - SUBLANES), SUBLANES)
        prev = jnp.where(seg_first, 0.0, z_ref[0, pl.ds(p0, SUBLANES), 0:W])
        nxt = jnp.where(seg_last, 0.0, z_ref[0, pl.ds(n0, SUBLANES), 0:W])
        ext = jnp.concatenate([prev, z_ref[0, pl.ds(r0, TILE), 0:W], nxt], axis=0)
        n_ext = TILE + 2 * SUBLANES
        xc = cb_ref[...]
        for k in range(LRU_CONV):
            off = k - LRU_CONV_LEFT
            sh = ext if off == 0 else pltpu.roll(ext, (-off) % n_ext, 0)
            xc = xc + cw_ref[k:k + 1, :] * sh[SUBLANES:SUBLANES + TILE]
        pre = _mm(xc, wg_ref[...]) + bg_ref[...]
        for d in range(2):
            r = jax.nn.sigmoid(pre[:, 2 * d * W:(2 * d + 1) * W])
            gate_i = jax.nn.sigmoid(pre[:, (2 * d + 1) * W:(2 * d + 2) * W])
            log_a = (-LRU_C) * r * sp_ref[d:d + 1, :]
            a2 = jnp.exp(2.0 * log_a)
            mult = jnp.sqrt((1.0 + a2) * jnp.tanh(-log_a))
            a_refs[d][pl.ds(r0, TILE), :] = jnp.exp(log_a)
            b_refs[d][pl.ds(r0, TILE), :] = mult * gate_i * xc
        return carry

    lax.fori_loop(0, nt, coeff_tile, 0)

    row = lax.broadcasted_iota(jnp.int32, (SCAN_ROWS, W), 0)
    n_chunks = T // SCAN_ROWS
    ctx_chunks = n_ctx_tiles * TILE // SCAN_ROWS

    def scan_step(j, carry):
        h_f, h_b = carry
        jf = pl.multiple_of(j * SCAN_ROWS, SCAN_ROWS)
        a = af_ref[pl.ds(jf, SCAN_ROWS), :]
        b = bf_ref[pl.ds(jf, SCAN_ROWS), :]
        for s in (1, 2, 4):
            m = row >= s
            b = jnp.where(m, a * pltpu.roll(b, s, 0) + b, b)
            a = jnp.where(m, a * pltpu.roll(a, s, 0), a)
        h = b + a * h_f
        hf_ref[pl.ds(jf, SCAN_ROWS), :] = h
        h_f = jnp.broadcast_to(h[SCAN_ROWS - 1:SCAN_ROWS, :], (SCAN_ROWS, W))

        cb = jnp.where(j < ctx_chunks, ctx_chunks - 1 - j, n_chunks - 1 + ctx_chunks - j)
        jb = pl.multiple_of(cb * SCAN_ROWS, SCAN_ROWS)
        a = ab_ref[pl.ds(jb, SCAN_ROWS), :]
        b = bb_ref[pl.ds(jb, SCAN_ROWS), :]
        for s in (1, 2, 4):
            m = row < SCAN_ROWS - s
            b = jnp.where(m, a * pltpu.roll(b, SCAN_ROWS - s, 0) + b, b)
            a = jnp.where(m, a * pltpu.roll(a, SCAN_ROWS - s, 0), a)
        h = b + a * h_b
        hb_ref[pl.ds(jb, SCAN_ROWS), :] = h
        h_b = jnp.broadcast_to(h[0:1, :], (SCAN_ROWS, W))
        return h_f, h_b

    zero = jnp.zeros((SCAN_ROWS, W), F32)
    lax.fori_loop(0, n_chunks, scan_step, (zero, zero))

    def out_tile(c, carry):
        r0 = pl.multiple_of(c * TILE, TILE)
        g = z_ref[0, pl.ds(r0, TILE), W:2 * W]
        y_ref[0, pl.ds(r0, TILE), :] = (hf_ref[pl.ds(r0, TILE), :] + hb_ref[pl.ds(r0, TILE), :]) * _gelu_tanh(g)
        return carry

    lax.fori_loop(0, nt, out_tile, 0)


def _lru(z, cw, cb, wg, bg, sp, n_ctx_tiles):
    B, T, _ = z.shape
    W = LRU_WIDTH
    return pl.pallas_call(
        functools.partial(_lru_kernel, n_ctx_tiles=n_ctx_tiles),
        grid=(B,),
        in_specs=[pl.BlockSpec((1, T, 2 * W), lambda bi: (bi, 0, 0)),
                  _const_spec(cw.shape), _const_spec(cb.shape), _const_spec(wg.shape),
                  _const_spec(bg.shape), _const_spec(sp.shape)],
        out_specs=pl.BlockSpec((1, T, W), lambda bi: (bi, 0, 0)),
        out_shape=jax.ShapeDtypeStruct((B, T, W), F32),
        scratch_shapes=[pltpu.VMEM((T, W), F32) for _ in range(6)],
        compiler_params=_params(("parallel",)),
        name="rglru",
    )(z, cw, cb, wg, bg, sp)


def _head_q(q, lo, hi):
    return jnp.where(_lane_mask(q.shape[-1], lo, hi), q.astype(F32), 0.0).astype(MXU_DTYPE)


def _fold_slabs(x, op):
    slabs = [x[r:r + SUBLANES] for r in range(0, x.shape[0], SUBLANES)]
    while len(slabs) > 1:
        nxt = [op(slabs[i], slabs[i + 1]) for i in range(0, len(slabs) - 1, 2)]
        if len(slabs) % 2:
            nxt.append(slabs[-1])
        slabs = nxt
    return slabs[0]


def _attend_maps(maps, s_refs):
    n_maps = len(maps)
    results = []
    m_prev = None
    for n in range(n_maps + 1):
        macc, o_t = None, None
        n_chunks = len(maps[min(n, n_maps - 1)][1])
        off = 0
        for c in range(n_chunks):
            if n < n_maps:
                qm, chunks = maps[n]
                k_fn, bias_fn, _ = chunks[c]
                s = _mm_nt(k_fn(), qm)
                if bias_fn is not None:
                    s = s + bias_fn()
                rows = s.shape[0]
                s_refs[n % 2][off:off + rows, :] = s
                mx = _fold_slabs(s, jnp.maximum)
                macc = mx if macc is None else jnp.maximum(macc, mx)
            if n >= 1:
                _, chunks = maps[n - 1]
                vt = chunks[c][2]()
                rows = vt.shape[1]
                vt = jnp.concatenate([vt, jnp.ones((DENOM_ROWS, rows), vt.dtype)], axis=0)
                e = jnp.exp2(s_refs[(n - 1) % 2][off:off + rows, :] - m_prev)
                o = jnp.dot(vt, e.astype(MXU_DTYPE), preferred_element_type=F32)
                o_t = o if o_t is None else o_t + o
            off += rows
        if n >= 1:
            dv = o_t.shape[0] - DENOM_ROWS
            results.append((o_t[0:dv], o_t[dv:dv + 1]))
        if n < n_maps:
            m_prev = jnp.max(macc, axis=0, keepdims=True)
    return results


def _key_chunks(k_ref, vt_ref, v_rows, nk):
    return [(lambda r0=r0, n=min(KEY_CHUNK, nk - r0): k_ref[0, r0:r0 + n, :], None,
             lambda r0=r0, n=min(KEY_CHUNK, nk - r0): vt_ref[0, v_rows, r0:r0 + n])
            for r0 in range(0, nk, KEY_CHUNK)]


def _diff_kernel(lam_ref, q_ref, k_ref, vt_ref, g_ref, o_ref, s0_ref, s1_ref, *, n_ctx):
    def attend(nk):
        q = q_ref[0]
        lam = lam_ref[...]
        maps = []
        for h in range(DIFF_HEADS):
            chunks = _key_chunks(k_ref, vt_ref, slice(h * DIFF_V, (h + 1) * DIFF_V), nk)
            for mth in range(2):
                c0 = (2 * h + mth) * DIFF_D
                maps.append((_head_q(q, c0, c0 + DIFF_D), chunks))
        res = _attend_maps(maps, (s0_ref, s1_ref))
        heads = []
        for h in range(DIFF_HEADS):
            (o1, l1), (o2, l2) = res[2 * h], res[2 * h + 1]
            o = o1 * (1.0 / l1) - o2 * (lam / l2)
            ms = jnp.mean(o * o, axis=0, keepdims=True)
            heads.append(o * lax.rsqrt(ms + EPS))
        o_ref[0] = jnp.concatenate(heads, axis=0).T * g_ref[...]

    i = pl.program_id(1)

    @pl.when(i == 0)
    def _():
        attend(n_ctx)

    @pl.when(i > 0)
    def _():
        attend(k_ref.shape[1])


def _diff_attn(lam, q, k, vt, g, n_ctx):
    B, T, W = q.shape
    return pl.pallas_call(
        functools.partial(_diff_kernel, n_ctx=n_ctx),
        grid=(B, T // TILE),
        in_specs=[_const_spec(lam.shape),
                  pl.BlockSpec((1, TILE, W), lambda bi, i: (bi, i, 0)),
                  pl.BlockSpec((1, T, W), lambda bi, i: (bi, 0, 0)),
                  pl.BlockSpec((1, W, T), lambda bi, i: (bi, 0, 0)),
                  _const_spec(g.shape)],
        out_specs=pl.BlockSpec((1, TILE, W), lambda bi, i: (bi, i, 0)),
        out_shape=jax.ShapeDtypeStruct((B, T, W), F32),
        scratch_shapes=[pltpu.VMEM((T, TILE), F32), pltpu.VMEM((T, TILE), F32)],
        compiler_params=_params(("parallel", "arbitrary")),
        name="diff_attn",
    )(lam, q, k, vt, g)


def _gqa_kernel(q_ref, k_ref, vt_ref, o_ref, s0_ref, s1_ref, *, n_ctx):
    def attend(nk):
        kvw = k_ref.shape[-1]
        group = GQA_Q_HEADS // GQA_KV_HEADS
        maps = []
        for h in range(GQA_Q_HEADS):
            j, half = h // group, h % group
            q = q_ref[0, :, half * kvw:(half + 1) * kvw]
            chunks = _key_chunks(k_ref, vt_ref, slice(j * GQA_D, (j + 1) * GQA_D), nk)
            maps.append((_head_q(q, j * GQA_D, (j + 1) * GQA_D), chunks))
        res = _attend_maps(maps, (s0_ref, s1_ref))
        o_ref[0] = jnp.concatenate([o * (1.0 / l) for o, l in res], axis=0).T

    i = pl.program_id(1)

    @pl.when(i == 0)
    def _():
        attend(n_ctx)

    @pl.when(i > 0)
    def _():
        attend(k_ref.shape[1])


def _gqa_attn(q, k, vt, n_ctx):
    B, T, W = q.shape
    KW = k.shape[-1]
    return pl.pallas_call(
        functools.partial(_gqa_kernel, n_ctx=n_ctx),
        grid=(B, T // TILE),
        in_specs=[pl.BlockSpec((1, TILE, W), lambda bi, i: (bi, i, 0)),
                  pl.BlockSpec((1, T, KW), lambda bi, i: (bi, 0, 0)),
                  pl.BlockSpec((1, KW, T), lambda bi, i: (bi, 0, 0))],
        out_specs=pl.BlockSpec((1, TILE, W), lambda bi, i: (bi, i, 0)),
        out_shape=jax.ShapeDtypeStruct((B, T, W), F32),
        scratch_shapes=[pltpu.VMEM((T, TILE), F32), pltpu.VMEM((T, TILE), F32)],
        compiler_params=_params(("parallel", "arbitrary")),
        name="gqa_attn",
    )(q, k, vt)


def _na_kernel(q_ref, kc_ref, k0_ref, k1_ref, k2_ref, vc_ref, v0_ref, v1_ref, v2_ref, bias_ref, o_ref,
               s0_ref, s1_ref):
    i = pl.program_id(1)

    def run(windowed):
        q = q_ref[0]
        maps = []
        for h in range(NA_HEADS):
            rows = slice(h * NA_D, (h + 1) * NA_D)
            chunks = [(lambda: kc_ref[0], None, lambda rows=rows: vc_ref[0, rows, :])]
            if windowed:
                for t, (kw_ref, vw_ref) in enumerate(((k0_ref, v0_ref), (k1_ref, v1_ref), (k2_ref, v2_ref))):
                    chunks.append((lambda kw_ref=kw_ref: kw_ref[0],
                                   lambda h=h, t=t: bias_ref[0, h, t * TILE:(t + 1) * TILE, :],
                                   lambda rows=rows, vw_ref=vw_ref: vw_ref[0, rows, :]))
            maps.append((_head_q(q, h * NA_D, (h + 1) * NA_D), chunks))
        res = _attend_maps(maps, (s0_ref, s1_ref))
        o_ref[0] = jnp.concatenate([o * (1.0 / l) for o, l in res], axis=0).T

    @pl.when(i == 0)
    def _():
        run(False)

    @pl.when(i > 0)
    def _():
        run(True)


def _na_attn(q, k, vt, bias, n_ctx_tiles, l):
    B, T, W = q.shape
    nt = T // TILE
    n_lat_tiles = nt - n_ctx_tiles

    def key_tile0(i):
        first = (i - n_ctx_tiles) - (NA_WIN_ROWS // 2) // NA_Q_ROWS
        return n_ctx_tiles + jnp.clip(first, 0, n_lat_tiles - NA_K_TILES)

    def bias_idx(bi, i):
        kind = jnp.where(i <= n_ctx_tiles, 0, jnp.where(i == nt - 1, 2, 1))
        return (l, kind, 0, 0, 0)

    k_spec = lambda t: pl.BlockSpec((1, TILE, W), lambda bi, i: (bi, key_tile0(i) + t, 0))
    v_spec = lambda t: pl.BlockSpec((1, W, TILE), lambda bi, i: (bi, 0, key_tile0(i) + t))
    return pl.pallas_call(
        _na_kernel,
        grid=(B, nt),
        in_specs=[pl.BlockSpec((1, TILE, W), lambda bi, i: (bi, i, 0)),
                  pl.BlockSpec((1, TILE, W), lambda bi, i: (bi, 0, 0)), k_spec(0), k_spec(1), k_spec(2),
                  pl.BlockSpec((1, W, TILE), lambda bi, i: (bi, 0, 0)), v_spec(0), v_spec(1), v_spec(2),
                  pl.BlockSpec((None, 1) + tuple(bias.shape[2:]), bias_idx)],
        out_specs=pl.BlockSpec((1, TILE, W), lambda bi, i: (bi, i, 0)),
        out_shape=jax.ShapeDtypeStruct((B, T, W), F32),
        scratch_shapes=[pltpu.VMEM(((NA_K_TILES + 1) * TILE, TILE), F32) for _ in range(2)],
        compiler_params=_params(("parallel", "arbitrary")),
        name="na_attn",
    )(q, k, k, k, k, vt, vt, vt, vt, bias)


def _na_bias_table(rpb, n_lat_rows):
    L, H, n_dr, n_dc = rpb.shape
    k_rows = NA_K_TILES * NA_Q_ROWS
    ext = jnp.pad(rpb * LOG2E, ((0, 0), (0, 0), (0, 1), (0, 1)), constant_values=NEG_BIG)
    a = np.arange(NA_Q_ROWS)[:, None]
    j = np.arange(k_rows)[None, :]
    c = np.arange(GRID_W)[:, None]
    c2 = np.arange(GRID_W)[None, :]
    c0 = np.clip(c - NA_WIN_COLS // 2, 0, GRID_W - NA_WIN_COLS)
    col_ok = (c2 >= c0) & (c2 < c0 + NA_WIN_COLS)
    col_sel = np.eye(n_dc + 1, dtype=np.float32)[np.where(col_ok, c2 - c + NA_WIN_COLS - 1, n_dc)]
    half = NA_WIN_ROWS // 2
    last_off = n_lat_rows - NA_WIN_ROWS - (n_lat_rows - k_rows)
    last_rel = (n_lat_rows - NA_Q_ROWS) - (n_lat_rows - k_rows)
    tables = []
    for rel, lo in ((0, 0 * a), (half, a), (last_rel, last_off + 0 * a)):
        row_ok = (j >= lo) & (j < lo + NA_WIN_ROWS)
        row_sel = np.eye(n_dr + 1, dtype=np.float32)[np.where(row_ok, j - rel - a + NA_WIN_ROWS - 1, n_dr)]
        t = jnp.einsum('ajr,lhrs->lhajs', jnp.asarray(row_sel), ext, precision=lax.Precision.HIGHEST)
        t = jnp.einsum('lhajs,cks->lhjkac', t, jnp.asarray(col_sel), precision=lax.Precision.HIGHEST)
        tables.append(t.reshape(L, H, k_rows * GRID_W, NA_Q_ROWS * GRID_W))
    return jnp.stack(tables, axis=1)


def _merge_kernel(ya_ref, yb_ref, yc_ref, yd_ref, gates_ref, x_ref, mod_ref, wb_ref, wo_ref, bo_ref,
                  g_ref, b_ref, o_ref):
    D = x_ref.shape[-1]
    m = jnp.zeros((x_ref.shape[1], D), F32)
    for j, y_ref in enumerate((ya_ref, yb_ref, yc_ref, yd_ref)):
        gate = gates_ref[0, :, j * D:(j + 1) * D].astype(F32)
        m = m + gate * _mm(y_ref[0], wb_ref[j])
    out = _mm(m, wo_ref[...]) + bo_ref[...]
    g1 = mod_ref[0][:, 2 * D:3 * D]
    o_ref[0] = _ln(ALPHA * x_ref[0] + g1 * out) * g_ref[...] + b_ref[...]


def _merge(ys, gates, x, modt, wb, wo, bo, g, b, skip_tiles, l):
    B, T, D = x.shape
    nt = T // TILE - skip_tiles
    tok = lambda width: pl.BlockSpec((1, TILE, width), lambda bi, i: (bi, i + skip_tiles, 0))
    return pl.pallas_call(
        _merge_kernel,
        grid=(B, nt),
        in_specs=[tok(BRANCH_W)] * 4 + [tok(gates.shape[-1]), tok(D),
                  pl.BlockSpec((1, 1, 6 * D), lambda bi, i: (2 * bi + jnp.minimum(i + skip_tiles, 1), 0, 0)),
                  _layer_spec(wb.shape, l), _layer_spec(wo.shape, l), _const_spec(bo.shape),
                  _const_spec(g.shape), _const_spec(b.shape)],
        out_specs=pl.BlockSpec((1, TILE, D), lambda bi, i: (bi, i, 0)),
        out_shape=jax.ShapeDtypeStruct((B, nt * TILE, D), F32),
        compiler_params=_params(("parallel", "arbitrary")),
        name="merge",
    )(*ys, gates, x, modt, wb, wo, bo, g, b)


def _ffn_kernel(xp_ref, x_ref, xn_ref, mod_ref, w1_ref, b1_ref, cw_ref, cb_ref, w2_ref, b2_ref,
                g_ref, b_ref, o_ref, *, seq_first, seq_last):
    D = x_ref.shape[-1]
    d_ff = w2_ref.shape[0]
    i = pl.program_id(1)
    has_prev = functools.reduce(jnp.logical_and, [i != t for t in seq_first])
    has_next = functools.reduce(jnp.logical_and, [i != t for t in seq_last])
    mod = mod_ref[0]
    x = x_ref[0]
    n_ext = TILE + 2 * SUBLANES
    xe = jnp.concatenate([xp_ref[0], x, xn_ref[0]], axis=0)
    h = (_ln(xe) * (1.0 + mod[:, 4 * D:5 * D]) + mod[:, 3 * D:4 * D]).astype(MXU_DTYPE)

    def up(lo, hi):
        u = jnp.dot(h, w1_ref[:, lo:hi], preferred_element_type=F32) + b1_ref[:, lo:hi]
        return jnp.concatenate([jnp.where(has_prev, u[0:SUBLANES], 0.0), u[SUBLANES:SUBLANES + TILE],
                                jnp.where(has_next, u[SUBLANES + TILE:], 0.0)], axis=0)

    def conv(u, lo, hi):
        y = (cw_ref[0:1, lo:hi] * pltpu.roll(u, 1, 0)[SUBLANES:SUBLANES + TILE]
             + cw_ref[1:2, lo:hi] * u[SUBLANES:SUBLANES + TILE]
             + cw_ref[2:3, lo:hi] * pltpu.roll(u, n_ext - 1, 0)[SUBLANES:SUBLANES + TILE])
        return y + cb_ref[:, lo:hi]

    chunks = [(lo, min(lo + FF_CHUNK, d_ff)) for lo in range(0, d_ff, FF_CHUNK)]
    acc = jnp.zeros((TILE, D), F32)
    pending = None
    for c in range(len(chunks) + 1):
        nxt = None
        if c < len(chunks):
            lo, hi = chunks[c]
            nxt = (up(lo, hi), up(d_ff + lo, d_ff + hi))
        if pending is not None:
            lo, hi = chunks[c - 1]
            val = conv(pending[0], lo, hi)
            gate = conv(pending[1], d_ff + lo, d_ff + hi)
            acc = acc + _mm(gate * jax.nn.sigmoid(gate) * val, w2_ref[lo:hi, :])
        pending = nxt
    f = acc + b2_ref[...]
    g2 = mod[:, 5 * D:6 * D]
    o_ref[0] = _ln(ALPHA * x + g2 * f) * g_ref[...] + b_ref[...]


def _ffn(x, modt, w1, b1, cw, cb, w2, b2, g, b, n_ctx_tiles, latent_only, l):
    B, T, D = x.shape
    nt = T // TILE
    rows8 = T // SUBLANES
    per = TILE // SUBLANES
    if latent_only:
        seq_first, seq_last = (0,), (nt - 1,)
        mod_idx = lambda bi, i: (2 * bi + 1, 0, 0)
    else:
        seq_first, seq_last = (0, n_ctx_tiles), (n_ctx_tiles - 1, nt - 1)
        mod_idx = lambda bi, i: (2 * bi + jnp.minimum(i, 1), 0, 0)
    return pl.pallas_call(
        functools.partial(_ffn_kernel, seq_first=seq_first, seq_last=seq_last),
        grid=(B, nt),
        in_specs=[pl.BlockSpec((1, SUBLANES, D), lambda bi, i: (bi, jnp.maximum(i * per - 1, 0), 0)),
                  pl.BlockSpec((1, TILE, D), lambda bi, i: (bi, i, 0)),
                  pl.BlockSpec((1, SUBLANES, D), lambda bi, i: (bi, jnp.minimum((i + 1) * per, rows8 - 1), 0)),
                  pl.BlockSpec((1, 1, 6 * D), mod_idx),
                  _layer_spec(w1.shape, l), _const_spec(b1.shape), _const_spec(cw.shape),
                  _const_spec(cb.shape), _layer_spec(w2.shape, l), _const_spec(b2.shape),
                  _const_spec(g.shape), _const_spec(b.shape)],
        out_specs=pl.BlockSpec((1, TILE, D), lambda bi, i: (bi, i, 0)),
        out_shape=jax.ShapeDtypeStruct((B, T, D), F32),
        compiler_params=_params(("parallel", "arbitrary")),
        name="conv_ffn",
    )(x, x, x, modt, w1, b1, cw, cb, w2, b2, g, b)


def _rope_tables(n_lat, n_ctx, dim, width):
    t = jnp.arange(n_lat)
    row = (t // GRID_W).astype(F32)
    col = (t % GRID_W).astype(F32)
    n_freq = dim // 4
    inv = ROPE_BASE ** (-jnp.arange(n_freq, dtype=F32) / n_freq)
    ang = jnp.concatenate([row[:, None] * inv, col[:, None] * inv], -1)
    cos, sin = jnp.cos(ang), jnp.sin(ang)
    cos = jnp.tile(jnp.concatenate([cos, cos], -1), (1, width // dim))
    sin = jnp.tile(jnp.concatenate([-sin, sin], -1), (1, width // dim))
    cos = jnp.concatenate([jnp.ones((n_ctx, width), F32), cos], 0)
    sin = jnp.concatenate([jnp.zeros((n_ctx, width), F32), sin], 0)
    return cos, sin


def _block_diag(w):
    nb, bw, _ = w.shape
    return jnp.einsum('ncd,nm->ncmd', w, jnp.eye(nb, dtype=w.dtype)).reshape(nb * bw, nb * bw)


_GQA_HEAD_ORDER = (0, 2, 1, 3)


def _permute_heads(a, axis, order, d):
    shp = a.shape
    a = a.reshape(shp[:axis] + (len(order), d) + shp[axis + 1:])
    a = jnp.take(a, jnp.asarray(order), axis=axis)
    return a.reshape(shp)


def kernel(x, c, ctx, c_ctx, w_ada, b_ada, w_in, b_in, lru_conv_w, lru_conv_b, lru_w_a, lru_b_a, lru_w_x, lru_b_x, lru_lambda, diff_lambda, diff_subln, gqa_q_norm, gqa_k_norm, na_rpb, w_branch, w_out, b_out, ln1_g, ln1_b, ffn_w1, ffn_b1, ffn_conv_w, ffn_conv_b, ffn_w2, ffn_b2, ln2_g, ln2_b):
    B, S, D = x.shape
    C = ctx.shape[1]
    L = w_ada.shape[0]
    assert C == TILE and S % TILE == 0 and S // TILE >= NA_K_TILES
    n_ctx_tiles = C // TILE

    pad = (-(B + 1)) % SUBLANES
    cvec = jnp.concatenate([c, c_ctx[None], jnp.zeros((pad, D), F32)], 0)
    mod_all = _ada(cvec, w_ada, b_ada)
    mod_lat = mod_all[:, :B]
    mod_ctx = jnp.broadcast_to(mod_all[:, B:B + 1], mod_lat.shape)
    modt = jnp.stack([mod_ctx, mod_lat], axis=2).reshape(L, 2 * B, 1, 6 * D)

    cd, sd = _rope_tables(S, C, DIFF_D, 256)
    cg, sg = _rope_tables(S, C, GQA_D, 256)
    seg64 = jnp.asarray(np.kron(np.eye(4), np.full((64, 64), 1.0 / 64)), MXU_DTYPE)

    o = _IN_OFFS
    gq_cols = _permute_heads(w_in[:, :, o[5]:o[6]], 2, _GQA_HEAD_ORDER, GQA_D)
    w_in_k = jnp.concatenate([w_in[:, :, :o[5]], gq_cols, w_in[:, :, o[6]:]], 2).astype(MXU_DTYPE)
    gq_b = _permute_heads(b_in[:, o[5]:o[6]], 1, _GQA_HEAD_ORDER, GQA_D)
    b_in_k = jnp.concatenate([b_in[:, :o[5]], gq_b, b_in[:, o[6]:]], 1)[:, None, :]
    wb_k = w_branch.astype(MXU_DTYPE)
    wo_k = w_out.astype(MXU_DTYPE)
    w1_k = ffn_w1.astype(MXU_DTYPE)
    w2_k = ffn_w2.astype(MXU_DTYPE)
    qn = jnp.tile(gqa_q_norm, (1, GQA_Q_HEADS))[:, None, :]
    kn = jnp.tile(gqa_k_norm, (1, GQA_KV_HEADS))[:, None, :]
    wg = jnp.concatenate([jax.vmap(_block_diag)(lru_w_a[:, 0]), jax.vmap(_block_diag)(lru_w_x[:, 0]),
                          jax.vmap(_block_diag)(lru_w_a[:, 1]), jax.vmap(_block_diag)(lru_w_x[:, 1])],
                         -1).astype(MXU_DTYPE)
    bg = jnp.concatenate([lru_b_a[:, 0].reshape(L, -1), lru_b_x[:, 0].reshape(L, -1),
                          lru_b_a[:, 1].reshape(L, -1), lru_b_x[:, 1].reshape(L, -1)], -1)[:, None, :]
    sp = jax.nn.softplus(-lru_lambda.astype(F32))
    lp = diff_lambda.astype(F32)
    lam_delta = jnp.exp(jnp.sum(lp[:, 0] * lp[:, 1], -1)) - jnp.exp(jnp.sum(lp[:, 2] * lp[:, 3], -1))
    na_bias = _na_bias_table(na_rpb.astype(F32), S // GRID_W)

    xs = jnp.concatenate([ctx, x], axis=1)
    for l in range(L):
        last = l == L - 1
        lam_init = 0.8 - 0.6 * math.exp(-0.3 * l)
        lam = (lam_delta[l] + lam_init).reshape(1, 1)
        subln = (jnp.tile(diff_subln[l], DIFF_HEADS) * (1.0 - lam_init))[None, :]

        (z_lru, dq, dk, dv, gq, gk, gv, nq, nk, nv, gates) = _in_proj(
            xs, modt[l], w_in_k, b_in_k, cd, sd, cg, sg, seg64, qn[l], kn[l], l)
        y_a = _lru(z_lru, lru_conv_w[l], lru_conv_b[l][None, :], wg[l], bg[l], sp[l], n_ctx_tiles)
        y_b = _diff_attn(lam, dq, dk, dv, subln, C)
        y_c = _gqa_attn(gq, gk, gv, C)
        y_d = _na_attn(nq, nk, nv, na_bias, n_ctx_tiles, l)

        skip = n_ctx_tiles if last else 0
        x1 = _merge((y_a, y_b, y_c, y_d), gates, xs, modt[l], wb_k, wo_k, b_out[l][None, :],
                    ln1_g[l][None, :], ln1_b[l][None, :], skip, l)
        xs = _ffn(x1, modt[l], w1_k, ffn_b1[l][None, :], ffn_conv_w[l], ffn_conv_b[l][None, :],
                  w2_k, ffn_b2[l][None, :], ln2_g[l][None, :], ln2_b[l][None, :], n_ctx_tiles, last, l)
    return xs
```

```python
import functools
import math

import numpy as np
import jax
import jax.numpy as jnp
from jax import lax
from jax.experimental import pallas as pl
from jax.experimental.pallas import tpu as pltpu

F32 = jnp.float32
MXU_DTYPE = jnp.bfloat16

DEPTH = 4
GRID_W = 64
LRU_WIDTH = 256
LRU_BLOCKS = 4
LRU_C = 8.0
LRU_CONV = 4
LRU_CONV_LEFT = 2
DIFF_HEADS = 4
DIFF_D = 32
DIFF_V = 64
GQA_Q_HEADS = 4
GQA_KV_HEADS = 2
GQA_D = 64
NA_HEADS = 4
NA_D = 64
NA_WIN_ROWS = 8
NA_WIN_COLS = 16
N_BRANCH = 4
BRANCH_W = 256
FFN_CONV = 3
ROPE_BASE = 10000.0
ALPHA = (2 * DEPTH) ** 0.25
EPS = 1e-6
NEG_BIG = -1e30

TILE = 256
SUBLANES = 8
NA_Q_ROWS = TILE // GRID_W
NA_K_TILES = 3
KEY_CHUNK = 512
DENOM_ROWS = 16
LOG2E = 1.4426950408889634
FF_CHUNK = 1408
SCAN_ROWS = 8
V7X_VMEM_LIMIT = 56 * 1024 * 1024

_IN_WIDTHS = (256, 256, 256, 256, 256, 256, 128, 128, 256, 256, 256)
_IN_OFFS = tuple(int(v) for v in np.cumsum((0,) + _IN_WIDTHS))


def _params(sem):
    return pltpu.CompilerParams(dimension_semantics=sem, vmem_limit_bytes=V7X_VMEM_LIMIT)


def _const_spec(shape):
    nd = len(shape)
    return pl.BlockSpec(shape, lambda *_: (0,) * nd)


def _layer_spec(shape, l):
    nd = len(shape)
    return pl.BlockSpec((None,) + tuple(shape[1:]), lambda *_: (l,) + (0,) * (nd - 1))


def _ln(x):
    mu = jnp.mean(x, axis=-1, keepdims=True)
    xc = x - mu
    var = jnp.mean(xc * xc, axis=-1, keepdims=True)
    return xc * lax.rsqrt(var + EPS)


def _mm(a, b):
    return jnp.dot(a.astype(MXU_DTYPE), b, preferred_element_type=F32)


def _mm_nt(a, b):
    return lax.dot_general(a, b, (((1,), (1,)), ((), ())), preferred_element_type=F32)


def _seg_mean(x2, seg_ref):
    hi = x2.astype(MXU_DTYPE)
    lo = (x2 - hi.astype(F32)).astype(MXU_DTYPE)
    seg = seg_ref[...]
    return (jnp.dot(hi, seg, preferred_element_type=F32)
            + jnp.dot(lo, seg, preferred_element_type=F32))


def _lane_mask(width, lo, hi):
    lane = lax.broadcasted_iota(jnp.int32, (1, width), 1)
    return (lane >= lo) & (lane < hi)


def _rope(x, cos, sin_signed, dim):
    n = x.shape[-1]
    half = dim // 2
    lane = lax.broadcasted_iota(jnp.int32, (1, n), 1)
    first = (lane & (dim - 1)) < half
    swapped = jnp.where(first, pltpu.roll(x, n - half, 1), pltpu.roll(x, half, 1))
    return x * cos + swapped * sin_signed


def _ada_kernel(c_ref, w_ref, b_ref, o_ref):
    c = c_ref[...]
    s = c * jax.nn.sigmoid(c)
    o_ref[0] = _mm(s, w_ref[0].astype(MXU_DTYPE)) + b_ref[0]


def _ada(cvec, w_ada, b_ada):
    L, D, N = w_ada.shape
    tn = 1536
    R = cvec.shape[0]
    return pl.pallas_call(
        _ada_kernel,
        grid=(L, N // tn),
        in_specs=[pl.BlockSpec((R, D), lambda l, j: (0, 0)),
                  pl.BlockSpec((1, D, tn), lambda l, j: (l, 0, j)),
                  pl.BlockSpec((1, 1, tn), lambda l, j: (l, 0, j))],
        out_specs=pl.BlockSpec((1, R, tn), lambda l, j: (l, 0, j)),
        out_shape=jax.ShapeDtypeStruct((L, R, N), F32),
        compiler_params=_params(("parallel", "parallel")),
        name="ada_mod",
    )(cvec, w_ada, b_ada.reshape(L, 1, N))


def _in_kernel(x_ref, mod_ref, w_ref, b_ref, cd_ref, sd_ref, cg_ref, sg_ref, seg_ref, qn_ref, kn_ref,
               lru_ref, dq_ref, dk_ref, dv_ref, gq_ref, gk_ref, gv_ref, nq_ref, nk_ref, nv_ref,
               gates_ref):
    D = x_ref.shape[-1]
    mod = mod_ref[0]
    h = (_ln(x_ref[0]) * (1.0 + mod[:, D:2 * D]) + mod[:, 0:D]).astype(MXU_DTYPE)

    def proj(lo, hi):
        return jnp.dot(h, w_ref[:, lo:hi], preferred_element_type=F32) + b_ref[:, lo:hi]

    o = _IN_OFFS
    lru_ref[0] = proj(o[0], o[2])

    cd, sd = cd_ref[...], sd_ref[...]
    dq_ref[0] = (_rope(proj(o[2], o[3]), cd, sd, DIFF_D) * (DIFF_D ** -0.5 * LOG2E)).astype(dq_ref.dtype)
    dk_ref[0] = _rope(proj(o[3], o[4]), cd, sd, DIFF_D).astype(dk_ref.dtype)
    dv_ref[0] = proj(o[4], o[5]).T.astype(dv_ref.dtype)

    cg, sg = cg_ref[...], sg_ref[...]
    gq = proj(o[5], o[6])
    gq = gq * lax.rsqrt(_seg_mean(gq * gq, seg_ref) + EPS) * qn_ref[...]
    gq_ref[0] = (_rope(gq, cg, sg, GQA_D) * (GQA_D ** -0.5 * LOG2E)).astype(gq_ref.dtype)
    kw = o[7] - o[6]
    gk = proj(o[6], o[7])
    gk = gk * lax.rsqrt(_seg_mean(gk * gk, seg_ref.at[0:kw, 0:kw]) + EPS) * kn_ref[...]
    gk_ref[0] = _rope(gk, cg[:, 0:kw], sg[:, 0:kw], GQA_D).astype(gk_ref.dtype)
    gv_ref[0] = proj(o[7], o[8]).T.astype(gv_ref.dtype)

    nq_ref[0] = (proj(o[8], o[9]) * (NA_D ** -0.5 * LOG2E)).astype(nq_ref.dtype)
    nk_ref[0] = proj(o[9], o[10]).astype(nk_ref.dtype)
    nv_ref[0] = proj(o[10], o[11]).T.astype(nv_ref.dtype)

    n_gate = gates_ref.shape[-1]
    for j in range(N_BRANCH):
        w = n_gate // N_BRANCH
        gates_ref[0, :, j * w:(j + 1) * w] = jax.nn.sigmoid(
            proj(o[11] + j * w, o[11] + (j + 1) * w)).astype(gates_ref.dtype)


def _in_proj(x, modt, w, b, cd, sd, cg, sg, seg, qn, kn, l):
    B, T, D = x.shape
    nt = T // TILE
    n_gate = w.shape[-1] - _IN_OFFS[11]
    tok = lambda width: pl.BlockSpec((1, TILE, width), lambda bi, i: (bi, i, 0))
    tab = lambda width: pl.BlockSpec((TILE, width), lambda bi, i: (i, 0))
    widths = (512, 256, 256, 256, 256, 128, 128, 256, 256, 256)
    dtypes = (F32,) + (MXU_DTYPE,) * 9
    v_outs = (3, 6, 9)
    tok_t = lambda width: pl.BlockSpec((1, width, TILE), lambda bi, i: (bi, 0, i))
    out_shape = [jax.ShapeDtypeStruct((B, wd, T) if n in v_outs else (B, T, wd), dt)
                 for n, (wd, dt) in enumerate(zip(widths, dtypes))]
    out_shape.append(jax.ShapeDtypeStruct((B, T, n_gate), MXU_DTYPE))
    return pl.pallas_call(
        _in_kernel,
        grid=(B, nt),
        in_specs=[tok(D),
                  pl.BlockSpec((1, 1, 6 * D), lambda bi, i: (2 * bi + jnp.minimum(i, 1), 0, 0)),
                  _layer_spec(w.shape, l), _layer_spec(b.shape, l),
                  tab(256), tab(256), tab(256), tab(256),
                  _const_spec(seg.shape), _const_spec(qn.shape), _const_spec(kn.shape)],
        out_specs=[tok_t(wd) if n in v_outs else tok(wd) for n, wd in enumerate(widths)] + [tok(n_gate)],
        out_shape=out_shape,
        compiler_params=_params(("parallel", "arbitrary")),
        name="in_proj",
    )(x, modt, w, b, cd, sd, cg, sg, seg, qn, kn)


def _gelu_tanh(x):
    return 0.5 * x * (1.0 + jnp.tanh(0.7978845608028654 * (x + 0.044715 * (x * x * x))))


def _lru_kernel(z_ref, cw_ref, cb_ref, wg_ref, bg_ref, sp_ref, y_ref,
                af_ref, bf_ref, ab_ref, bb_ref, hf_ref, hb_ref, *, n_ctx_tiles):
    T = z_ref.shape[1]
    W = LRU_WIDTH
    nt = T // TILE
    last = nt - 1
    a_refs = (af_ref, ab_ref)
    b_refs = (bf_ref, bb_ref)

    def coeff_tile(c, carry):
        r0 = pl.multiple_of(c * TILE, TILE)
        seg_first = (c == 0) | (c == n_ctx_tiles)
        seg_last = (c == n_ctx_tiles - 1) | (c == last)
        p0 = pl.multiple_of(jnp.maximum(r0 - SUBLANES, 0), SUBLANES)
        n0 = pl.multiple_of(jnp.minimum(r0 + TILE, T - SUBLANES), SUBLANES)
        prev = jnp.where(seg_first, 0.0, z_ref[0, pl.ds(p0, SUBLANES), 0:W])
        nxt = jnp.where(seg_last, 0.0, z_ref[0, pl.ds(n0, SUBLANES), 0:W])
        ext = jnp.concatenate([prev, z_ref[0, pl.ds(r0, TILE), 0:W], nxt], axis=0)
        n_ext = TILE + 2 * SUBLANES
        xc = cb_ref[...]
        for k in range(LRU_CONV):
            off = k - LRU_CONV_LEFT
            sh = ext if off == 0 else pltpu.roll(ext, (-off) % n_ext, 0)
            xc = xc + cw_ref[k:k + 1, :] * sh[SUBLANES:SUBLANES + TILE]
        pre = _mm(xc, wg_ref[...]) + bg_ref[...]
        for d in range(2):
            r = jax.nn.sigmoid(pre[:, 2 * d * W:(2 * d + 1) * W])
            gate_i = jax.nn.sigmoid(pre[:, (2 * d + 1) * W:(2 * d + 2) * W])
            log_a = (-LRU_C) * r * sp_ref[d:d + 1, :]
            a2 = jnp.exp(2.0 * log_a)
            mult = jnp.sqrt((1.0 + a2) * jnp.tanh(-log_a))
            a_refs[d][pl.ds(r0, TILE), :] = jnp.exp(log_a)
            b_refs[d][pl.ds(r0, TILE), :] = mult * gate_i * xc
        return carry

    lax.fori_loop(0, nt, coeff_tile, 0)

    row = lax.broadcasted_iota(jnp.int32, (SCAN_ROWS, W), 0)
    n_chunks = T // SCAN_ROWS
    ctx_chunks = n_ctx_tiles * TILE // SCAN_ROWS

    def scan_step(j, carry):
        h_f, h_b = carry
        jf = pl.multiple_of(j * SCAN_ROWS, SCAN_ROWS)
        a = af_ref[pl.ds(jf, SCAN_ROWS), :]
        b = bf_ref[pl.ds(jf, SCAN_ROWS), :]
        for s in (1, 2, 4):
            m = row >= s
            b = jnp.where(m, a * pltpu.roll(b, s, 0) + b, b)
            a = jnp.where(m, a * pltpu.roll(a, s, 0), a)
        h = b + a * h_f
        hf_ref[pl.ds(jf, SCAN_ROWS), :] = h
        h_f = jnp.broadcast_to(h[SCAN_ROWS - 1:SCAN_ROWS, :], (SCAN_ROWS, W))

        cb = jnp.where(j < ctx_chunks, ctx_chunks - 1 - j, n_chunks - 1 + ctx_chunks - j)
        jb = pl.multiple_of(cb * SCAN_ROWS, SCAN_ROWS)
        a = ab_ref[pl.ds(jb, SCAN_ROWS), :]
        b = bb_ref[pl.ds(jb, SCAN_ROWS), :]
        for s in (1, 2, 4):
            m = row < SCAN_ROWS - s
            b = jnp.where(m, a * pltpu.roll(b, SCAN_ROWS - s, 0) + b, b)
            a = jnp.where(m, a * pltpu.roll(a, SCAN_ROWS - s, 0), a)
        h = b + a * h_b
        hb_ref[pl.ds(jb, SCAN_ROWS), :] = h
        h_b = jnp.broadcast_to(h[0:1, :], (SCAN_ROWS, W))
        return h_f, h_b

    zero = jnp.zeros((SCAN_ROWS, W), F32)
    lax.fori_loop(0, n_chunks, scan_step, (zero, zero))

    def out_tile(c, carry):
        r0 = pl.multiple_of(c * TILE, TILE)
        g = z_ref[0, pl.ds(r0, TILE), W:2 * W]
        y_ref[0, pl.ds(r0, TILE), :] = ((hf_ref[pl.ds(r0, TILE), :] + hb_ref[pl.ds(r0, TILE), :])
                                        * _gelu_tanh(g)).astype(y_ref.dtype)
        return carry

    lax.fori_loop(0, nt, out_tile, 0)


def _lru(z, cw, cb, wg, bg, sp, n_ctx_tiles):
    B, T, _ = z.shape
    W = LRU_WIDTH
    return pl.pallas_call(
        functools.partial(_lru_kernel, n_ctx_tiles=n_ctx_tiles),
        grid=(B,),
        in_specs=[pl.BlockSpec((1, T, 2 * W), lambda bi: (bi, 0, 0)),
                  _const_spec(cw.shape), _const_spec(cb.shape), _const_spec(wg.shape),
                  _const_spec(bg.shape), _const_spec(sp.shape)],
        out_specs=pl.BlockSpec((1, T, W), lambda bi: (bi, 0, 0)),
        out_shape=jax.ShapeDtypeStruct((B, T, W), MXU_DTYPE),
        scratch_shapes=[pltpu.VMEM((T, W), F32) for _ in range(6)],
        compiler_params=_params(("parallel",)),
        name="rglru",
    )(z, cw, cb, wg, bg, sp)


def _head_q(q, lo, hi):
    return jnp.where(_lane_mask(q.shape[-1], lo, hi), q.astype(F32), 0.0).astype(MXU_DTYPE)


def _fold_slabs(x, op):
    slabs = [x[r:r + SUBLANES] for r in range(0, x.shape[0], SUBLANES)]
    while len(slabs) > 1:
        nxt = [op(slabs[i], slabs[i + 1]) for i in range(0, len(slabs) - 1, 2)]
        if len(slabs) % 2:
            nxt.append(slabs[-1])
        slabs = nxt
    return slabs[0]


def _attend_maps(maps, s_refs):
    n_maps = len(maps)
    results = []
    m_prev = None
    for n in range(n_maps + 1):
        macc, o_t = None, None
        n_chunks = len(maps[min(n, n_maps - 1)][1])
        off = 0
        for c in range(n_chunks):
            if n < n_maps:
                qm, chunks = maps[n]
                k_fn, bias_fn, _ = chunks[c]
                s = _mm_nt(k_fn(), qm)
                if bias_fn is not None:
                    s = s + bias_fn()
                rows = s.shape[0]
                s_refs[n % 2][off:off + rows, :] = s
                mx = _fold_slabs(s, jnp.maximum)
                macc = mx if macc is None else jnp.maximum(macc, mx)
            if n >= 1:
                _, chunks = maps[n - 1]
                vt = chunks[c][2]()
                rows = vt.shape[1]
                vt = jnp.concatenate([vt, jnp.ones((DENOM_ROWS, rows), vt.dtype)], axis=0)
                e = jnp.exp2(s_refs[(n - 1) % 2][off:off + rows, :] - m_prev)
                o = jnp.dot(vt, e.astype(MXU_DTYPE), preferred_element_type=F32)
                o_t = o if o_t is None else o_t + o
            off += rows
        if n >= 1:
            dv = o_t.shape[0] - DENOM_ROWS
            results.append((o_t[0:dv], o_t[dv:dv + 1]))
        if n < n_maps:
            m_prev = jnp.max(macc, axis=0, keepdims=True)
    return results


def _key_chunks(k_ref, vt_ref, v_rows, nk):
    return [(lambda r0=r0, n=min(KEY_CHUNK, nk - r0): k_ref[0, r0:r0 + n, :], None,
             lambda r0=r0, n=min(KEY_CHUNK, nk - r0): vt_ref[0, v_rows, r0:r0 + n])
            for r0 in range(0, nk, KEY_CHUNK)]


def _diff_kernel(lam_ref, q_ref, k_ref, vt_ref, g_ref, o_ref, s0_ref, s1_ref, *, n_ctx):
    def attend(nk):
        q = q_ref[0]
        lam = lam_ref[...]
        maps = []
        for h in range(DIFF_HEADS):
            chunks = _key_chunks(k_ref, vt_ref, slice(h * DIFF_V, (h + 1) * DIFF_V), nk)
            for mth in range(2):
                c0 = (2 * h + mth) * DIFF_D
                maps.append((_head_q(q, c0, c0 + DIFF_D), chunks))
        res = _attend_maps(maps, (s0_ref, s1_ref))
        heads = []
        for h in range(DIFF_HEADS):
            (o1, l1), (o2, l2) = res[2 * h], res[2 * h + 1]
            o = o1 * (1.0 / l1) - o2 * (lam / l2)
            ms = jnp.mean(o * o, axis=0, keepdims=True)
            heads.append(o * lax.rsqrt(ms + EPS))
        o_ref[0] = (jnp.concatenate(heads, axis=0).T * g_ref[...]).astype(o_ref.dtype)

    i = pl.program_id(1)

    @pl.when(i == 0)
    def _():
        attend(n_ctx)

    @pl.when(i > 0)
    def _():
        attend(k_ref.shape[1])


def _diff_attn(lam, q, k, vt, g, n_ctx):
    B, T, W = q.shape
    return pl.pallas_call(
        functools.partial(_diff_kernel, n_ctx=n_ctx),
        grid=(B, T // TILE),
        in_specs=[_const_spec(lam.shape),
                  pl.BlockSpec((1, TILE, W), lambda bi, i: (bi, i, 0)),
                  pl.BlockSpec((1, T, W), lambda bi, i: (bi, 0, 0)),
                  pl.BlockSpec((1, W, T), lambda bi, i: (bi, 0, 0)),
                  _const_spec(g.shape)],
        out_specs=pl.BlockSpec((1, TILE, W), lambda bi, i: (bi, i, 0)),
        out_shape=jax.ShapeDtypeStruct((B, T, W), MXU_DTYPE),
        scratch_shapes=[pltpu.VMEM((T, TILE), F32), pltpu.VMEM((T, TILE), F32)],
        compiler_params=_params(("parallel", "arbitrary")),
        name="diff_attn",
    )(lam, q, k, vt, g)


def _gqa_kernel(q_ref, k_ref, vt_ref, o_ref, s0_ref, s1_ref, *, n_ctx):
    def attend(nk):
        kvw = k_ref.shape[-1]
        group = GQA_Q_HEADS // GQA_KV_HEADS
        maps = []
        for h in range(GQA_Q_HEADS):
            j, half = h // group, h % group
            q = q_ref[0, :, half * kvw:(half + 1) * kvw]
            chunks = _key_chunks(k_ref, vt_ref, slice(j * GQA_D, (j + 1) * GQA_D), nk)
            maps.append((_head_q(q, j * GQA_D, (j + 1) * GQA_D), chunks))
        res = _attend_maps(maps, (s0_ref, s1_ref))
        o_ref[0] = jnp.concatenate([o * (1.0 / l) for o, l in res], axis=0).T.astype(o_ref.dtype)

    i = pl.program_id(1)

    @pl.when(i == 0)
    def _():
        attend(n_ctx)

    @pl.when(i > 0)
    def _():
        attend(k_ref.shape[1])


def _gqa_attn(q, k, vt, n_ctx):
    B, T, W = q.shape
    KW = k.shape[-1]
    return pl.pallas_call(
        functools.partial(_gqa_kernel, n_ctx=n_ctx),
        grid=(B, T // TILE),
        in_specs=[pl.BlockSpec((1, TILE, W), lambda bi, i: (bi, i, 0)),
                  pl.BlockSpec((1, T, KW), lambda bi, i: (bi, 0, 0)),
                  pl.BlockSpec((1, KW, T), lambda bi, i: (bi, 0, 0))],
        out_specs=pl.BlockSpec((1, TILE, W), lambda bi, i: (bi, i, 0)),
        out_shape=jax.ShapeDtypeStruct((B, T, W), MXU_DTYPE),
        scratch_shapes=[pltpu.VMEM((T, TILE), F32), pltpu.VMEM((T, TILE), F32)],
        compiler_params=_params(("parallel", "arbitrary")),
        name="gqa_attn",
    )(q, k, vt)


def _na_kernel(q_ref, kc_ref, k0_ref, k1_ref, k2_ref, vc_ref, v0_ref, v1_ref, v2_ref, bias_ref, o_ref,
               s0_ref, s1_ref):
    i = pl.program_id(1)

    def run(windowed):
        q = q_ref[0]
        maps = []
        for h in range(NA_HEADS):
            rows = slice(h * NA_D, (h + 1) * NA_D)
            chunks = [(lambda: kc_ref[0], None, lambda rows=rows: vc_ref[0, rows, :])]
            if windowed:
                for t, (kw_ref, vw_ref) in enumerate(((k0_ref, v0_ref), (k1_ref, v1_ref), (k2_ref, v2_ref))):
                    chunks.append((lambda kw_ref=kw_ref: kw_ref[0],
                                   lambda h=h, t=t: bias_ref[0, h, t * TILE:(t + 1) * TILE, :],
                                   lambda rows=rows, vw_ref=vw_ref: vw_ref[0, rows, :]))
            maps.append((_head_q(q, h * NA_D, (h + 1) * NA_D), chunks))
        res = _attend_maps(maps, (s0_ref, s1_ref))
        o_ref[0] = jnp.concatenate([o * (1.0 / l) for o, l in res], axis=0).T.astype(o_ref.dtype)

    @pl.when(i == 0)
    def _():
        run(False)

    @pl.when(i > 0)
    def _():
        run(True)


def _na_attn(q, k, vt, bias, n_ctx_tiles, l):
    B, T, W = q.shape
    nt = T // TILE
    n_lat_tiles = nt - n_ctx_tiles

    def key_tile0(i):
        first = (i - n_ctx_tiles) - (NA_WIN_ROWS // 2) // NA_Q_ROWS
        return n_ctx_tiles + jnp.clip(first, 0, n_lat_tiles - NA_K_TILES)

    def bias_idx(bi, i):
        kind = jnp.where(i <= n_ctx_tiles, 0, jnp.where(i == nt - 1, 2, 1))
        return (l, kind, 0, 0, 0)

    k_spec = lambda t: pl.BlockSpec((1, TILE, W), lambda bi, i: (bi, key_tile0(i) + t, 0))
    v_spec = lambda t: pl.BlockSpec((1, W, TILE), lambda bi, i: (bi, 0, key_tile0(i) + t))
    return pl.pallas_call(
        _na_kernel,
        grid=(B, nt),
        in_specs=[pl.BlockSpec((1, TILE, W), lambda bi, i: (bi, i, 0)),
                  pl.BlockSpec((1, TILE, W), lambda bi, i: (bi, 0, 0)), k_spec(0), k_spec(1), k_spec(2),
                  pl.BlockSpec((1, W, TILE), lambda bi, i: (bi, 0, 0)), v_spec(0), v_spec(1), v_spec(2),
                  pl.BlockSpec((None, 1) + tuple(bias.shape[2:]), bias_idx)],
        out_specs=pl.BlockSpec((1, TILE, W), lambda bi, i: (bi, i, 0)),
        out_shape=jax.ShapeDtypeStruct((B, T, W), MXU_DTYPE),
        scratch_shapes=[pltpu.VMEM(((NA_K_TILES + 1) * TILE, TILE), F32) for _ in range(2)],
        compiler_params=_params(("parallel", "arbitrary")),
        name="na_attn",
    )(q, k, k, k, k, vt, vt, vt, vt, bias)


def _na_bias_table(rpb, n_lat_rows):
    L, H, n_dr, n_dc = rpb.shape
    k_rows = NA_K_TILES * NA_Q_ROWS
    ext = jnp.pad(rpb * LOG2E, ((0, 0), (0, 0), (0, 1), (0, 1)), constant_values=NEG_BIG)
    a = np.arange(NA_Q_ROWS)[:, None]
    j = np.arange(k_rows)[None, :]
    c = np.arange(GRID_W)[:, None]
    c2 = np.arange(GRID_W)[None, :]
    c0 = np.clip(c - NA_WIN_COLS // 2, 0, GRID_W - NA_WIN_COLS)
    col_ok = (c2 >= c0) & (c2 < c0 + NA_WIN_COLS)
    col_sel = np.eye(n_dc + 1, dtype=np.float32)[np.where(col_ok, c2 - c + NA_WIN_COLS - 1, n_dc)]
    half = NA_WIN_ROWS // 2
    last_off = n_lat_rows - NA_WIN_ROWS - (n_lat_rows - k_rows)
    last_rel = (n_lat_rows - NA_Q_ROWS) - (n_lat_rows - k_rows)
    tables = []
    for rel, lo in ((0, 0 * a), (half, a), (last_rel, last_off + 0 * a)):
        row_ok = (j >= lo) & (j < lo + NA_WIN_ROWS)
        row_sel = np.eye(n_dr + 1, dtype=np.float32)[np.where(row_ok, j - rel - a + NA_WIN_ROWS - 1, n_dr)]
        t = jnp.einsum('ajr,lhrs->lhajs', jnp.asarray(row_sel), ext, precision=lax.Precision.HIGHEST)
        t = jnp.einsum('lhajs,cks->lhjkac', t, jnp.asarray(col_sel), precision=lax.Precision.HIGHEST)
        tables.append(t.reshape(L, H, k_rows * GRID_W, NA_Q_ROWS * GRID_W))
    return jnp.stack(tables, axis=1)


def _merge_kernel(ya_ref, yb_ref, yc_ref, yd_ref, gates_ref, x_ref, mod_ref, wb_ref, wo_ref, bo_ref,
                  g_ref, b_ref, o_ref):
    D = x_ref.shape[-1]
    m = jnp.zeros((x_ref.shape[1], D), F32)
    for j, y_ref in enumerate((ya_ref, yb_ref, yc_ref, yd_ref)):
        gate = gates_ref[0, :, j * D:(j + 1) * D].astype(F32)
        m = m + gate * _mm(y_ref[0], wb_ref[j])
    out = _mm(m, wo_ref[...]) + bo_ref[...]
    g1 = mod_ref[0][:, 2 * D:3 * D]
    o_ref[0] = _ln(ALPHA * x_ref[0] + g1 * out) * g_ref[...] + b_ref[...]


def _merge(ys, gates, x, modt, wb, wo, bo, g, b, skip_tiles, l):
    B, T, D = x.shape
    nt = T // TILE - skip_tiles
    tok = lambda width: pl.BlockSpec((1, TILE, width), lambda bi, i: (bi, i + skip_tiles, 0))
    return pl.pallas_call(
        _merge_kernel,
        grid=(B, nt),
        in_specs=[tok(BRANCH_W)] * 4 + [tok(gates.shape[-1]), tok(D),
                  pl.BlockSpec((1, 1, 6 * D), lambda bi, i: (2 * bi + jnp.minimum(i + skip_tiles, 1), 0, 0)),
                  _layer_spec(wb.shape, l), _layer_spec(wo.shape, l), _const_spec(bo.shape),
                  _const_spec(g.shape), _const_spec(b.shape)],
        out_specs=pl.BlockSpec((1, TILE, D), lambda bi, i: (bi, i, 0)),
        out_shape=jax.ShapeDtypeStruct((B, nt * TILE, D), F32),
        compiler_params=_params(("parallel", "arbitrary")),
        name="merge",
    )(*ys, gates, x, modt, wb, wo, bo, g, b)


def _ffn_kernel(xp_ref, x_ref, xn_ref, mod_ref, w1_ref, b1_ref, cw_ref, cb_ref, w2_ref, b2_ref,
                g_ref, b_ref, o_ref, *, seq_first, seq_last):
    D = x_ref.shape[-1]
    d_ff = w2_ref.shape[0]
    i = pl.program_id(1)
    has_prev = functools.reduce(jnp.logical_and, [i != t for t in seq_first])
    has_next = functools.reduce(jnp.logical_and, [i != t for t in seq_last])
    mod = mod_ref[0]
    x = x_ref[0]
    n_ext = TILE + 2 * SUBLANES
    xe = jnp.concatenate([xp_ref[0], x, xn_ref[0]], axis=0)
    h = (_ln(xe) * (1.0 + mod[:, 4 * D:5 * D]) + mod[:, 3 * D:4 * D]).astype(MXU_DTYPE)

    row = lax.broadcasted_iota(jnp.int32, (SUBLANES, 1), 0)
    edge_first = (row == 0) & jnp.logical_not(has_prev)
    edge_last = (row == SUBLANES - 1) & jnp.logical_not(has_next)

    def up(lo, hi):
        u = jnp.dot(h, w1_ref[:, lo:hi], preferred_element_type=F32)
        return jnp.concatenate([jnp.where(has_prev, u[0:SUBLANES], 0.0), u[SUBLANES:SUBLANES + TILE],
                                jnp.where(has_next, u[SUBLANES + TILE:], 0.0)], axis=0)

    def conv(u, lo, hi):
        w0, w1, w2 = cw_ref[0:1, lo:hi], cw_ref[1:2, lo:hi], cw_ref[2:3, lo:hi]
        b1 = b1_ref[:, lo:hi]
        y = (w0 * pltpu.roll(u, 1, 0)[SUBLANES:SUBLANES + TILE]
             + w1 * u[SUBLANES:SUBLANES + TILE]
             + w2 * pltpu.roll(u, n_ext - 1, 0)[SUBLANES:SUBLANES + TILE]) + (cb_ref[:, lo:hi] + b1 * (w0 + w1 + w2))
        return jnp.concatenate([y[0:SUBLANES] - jnp.where(edge_first, b1 * w0, 0.0),
                                y[SUBLANES:TILE - SUBLANES],
                                y[TILE - SUBLANES:TILE] - jnp.where(edge_last, b1 * w2, 0.0)], axis=0)

    chunks = [(lo, min(lo + FF_CHUNK, d_ff)) for lo in range(0, d_ff, FF_CHUNK)]
    acc = jnp.zeros((TILE, D), F32)
    pending = None
    for c in range(len(chunks) + 1):
        nxt = None
        if c < len(chunks):
            lo, hi = chunks[c]
            nxt = (up(lo, hi), up(d_ff + lo, d_ff + hi))
        if pending is not None:
            lo, hi = chunks[c - 1]
            val = conv(pending[0], lo, hi)
            gate = conv(pending[1], d_ff + lo, d_ff + hi)
            acc = acc + _mm(gate * jax.nn.sigmoid(gate) * val, w2_ref[lo:hi, :])
        pending = nxt
    f = acc + b2_ref[...]
    g2 = mod[:, 5 * D:6 * D]
    o_ref[0] = _ln(ALPHA * x + g2 * f) * g_ref[...] + b_ref[...]


def _ffn(x, modt, w1, b1, cw, cb, w2, b2, g, b, n_ctx_tiles, latent_only, l):
    B, T, D = x.shape
    nt = T // TILE
    rows8 = T // SUBLANES
    per = TILE // SUBLANES
    if latent_only:
        seq_first, seq_last = (0,), (nt - 1,)
        mod_idx = lambda bi, i: (2 * bi + 1, 0, 0)
    else:
        seq_first, seq_last = (0, n_ctx_tiles), (n_ctx_tiles - 1, nt - 1)
        mod_idx = lambda bi, i: (2 * bi + jnp.minimum(i, 1), 0, 0)
    return pl.pallas_call(
        functools.partial(_ffn_kernel, seq_first=seq_first, seq_last=seq_last),
        grid=(B, nt),
        in_specs=[pl.BlockSpec((1, SUBLANES, D), lambda bi, i: (bi, jnp.maximum(i * per - 1, 0), 0)),
                  pl.BlockSpec((1, TILE, D), lambda bi, i: (bi, i, 0)),
                  pl.BlockSpec((1, SUBLANES, D), lambda bi, i: (bi, jnp.minimum((i + 1) * per, rows8 - 1), 0)),
                  pl.BlockSpec((1, 1, 6 * D), mod_idx),
                  _layer_spec(w1.shape, l), _const_spec(b1.shape), _const_spec(cw.shape),
                  _const_spec(cb.shape), _layer_spec(w2.shape, l), _const_spec(b2.shape),
                  _const_spec(g.shape), _const_spec(b.shape)],
        out_specs=pl.BlockSpec((1, TILE, D), lambda bi, i: (bi, i, 0)),
        out_shape=jax.ShapeDtypeStruct((B, T, D), F32),
        compiler_params=_params(("parallel", "arbitrary")),
        name="conv_ffn",
    )(x, x, x, modt, w1, b1, cw, cb, w2, b2, g, b)


def _rope_tables(n_lat, n_ctx, dim, width):
    t = jnp.arange(n_lat)
    row = (t // GRID_W).astype(F32)
    col = (t % GRID_W).astype(F32)
    n_freq = dim // 4
    inv = ROPE_BASE ** (-jnp.arange(n_freq, dtype=F32) / n_freq)
    ang = jnp.concatenate([row[:, None] * inv, col[:, None] * inv], -1)
    cos, sin = jnp.cos(ang), jnp.sin(ang)
    cos = jnp.tile(jnp.concatenate([cos, cos], -1), (1, width // dim))
    sin = jnp.tile(jnp.concatenate([-sin, sin], -1), (1, width // dim))
    cos = jnp.concatenate([jnp.ones((n_ctx, width), F32), cos], 0)
    sin = jnp.concatenate([jnp.zeros((n_ctx, width), F32), sin], 0)
    return cos, sin


def _block_diag(w):
    nb, bw, _ = w.shape
    return jnp.einsum('ncd,nm->ncmd', w, jnp.eye(nb, dtype=w.dtype)).reshape(nb * bw, nb * bw)


_GQA_HEAD_ORDER = (0, 2, 1, 3)


def _permute_heads(a, axis, order, d):
    shp = a.shape
    a = a.reshape(shp[:axis] + (len(order), d) + shp[axis + 1:])
    a = jnp.take(a, jnp.asarray(order), axis=axis)
    return a.reshape(shp)


def kernel(x, c, ctx, c_ctx, w_ada, b_ada, w_in, b_in, lru_conv_w, lru_conv_b, lru_w_a, lru_b_a, lru_w_x, lru_b_x, lru_lambda, diff_lambda, diff_subln, gqa_q_norm, gqa_k_norm, na_rpb, w_branch, w_out, b_out, ln1_g, ln1_b, ffn_w1, ffn_b1, ffn_conv_w, ffn_conv_b, ffn_w2, ffn_b2, ln2_g, ln2_b):
    B, S, D = x.shape
    C = ctx.shape[1]
    L = w_ada.shape[0]
    assert C == TILE and S % TILE == 0 and S // TILE >= NA_K_TILES
    n_ctx_tiles = C // TILE

    pad = (-(B + 1)) % SUBLANES
    cvec = jnp.concatenate([c, c_ctx[None], jnp.zeros((pad, D), F32)], 0)
    mod_all = _ada(cvec, w_ada, b_ada)
    mod_lat = mod_all[:, :B]
    mod_ctx = jnp.broadcast_to(mod_all[:, B:B + 1], mod_lat.shape)
    modt = jnp.stack([mod_ctx, mod_lat], axis=2).reshape(L, 2 * B, 1, 6 * D)

    cd, sd = _rope_tables(S, C, DIFF_D, 256)
    cg, sg = _rope_tables(S, C, GQA_D, 256)
    seg64 = jnp.asarray(np.kron(np.eye(4), np.full((64, 64), 1.0 / 64)), MXU_DTYPE)

    o = _IN_OFFS
    gq_cols = _permute_heads(w_in[:, :, o[5]:o[6]], 2, _GQA_HEAD_ORDER, GQA_D)
    w_in_k = jnp.concatenate([w_in[:, :, :o[5]], gq_cols, w_in[:, :, o[6]:]], 2).astype(MXU_DTYPE)
    gq_b = _permute_heads(b_in[:, o[5]:o[6]], 1, _GQA_HEAD_ORDER, GQA_D)
    b_in_k = jnp.concatenate([b_in[:, :o[5]], gq_b, b_in[:, o[6]:]], 1)[:, None, :]
    wb_k = w_branch.astype(MXU_DTYPE)
    wo_k = w_out.astype(MXU_DTYPE)
    w1_k = ffn_w1.astype(MXU_DTYPE)
    w2_k = ffn_w2.astype(MXU_DTYPE)
    qn = jnp.tile(gqa_q_norm, (1, GQA_Q_HEADS))[:, None, :]
    kn = jnp.tile(gqa_k_norm, (1, GQA_KV_HEADS))[:, None, :]
    wg = jnp.concatenate([jax.vmap(_block_diag)(lru_w_a[:, 0]), jax.vmap(_block_diag)(lru_w_x[:, 0]),
                          jax.vmap(_block_diag)(lru_w_a[:, 1]), jax.vmap(_block_diag)(lru_w_x[:, 1])],
                         -1).astype(MXU_DTYPE)
    bg = jnp.concatenate([lru_b_a[:, 0].reshape(L, -1), lru_b_x[:, 0].reshape(L, -1),
                          lru_b_a[:, 1].reshape(L, -1), lru_b_x[:, 1].reshape(L, -1)], -1)[:, None, :]
    sp = jax.nn.softplus(-lru_lambda.astype(F32))
    lp = diff_lambda.astype(F32)
    lam_delta = jnp.exp(jnp.sum(lp[:, 0] * lp[:, 1], -1)) - jnp.exp(jnp.sum(lp[:, 2] * lp[:, 3], -1))
    na_bias = _na_bias_table(na_rpb.astype(F32), S // GRID_W)

    xs = jnp.concatenate([ctx, x], axis=1)
    for l in range(L):
        last = l == L - 1
        lam_init = 0.8 - 0.6 * math.exp(-0.3 * l)
        lam = (lam_delta[l] + lam_init).reshape(1, 1)
        subln = (jnp.tile(diff_subln[l], DIFF_HEADS) * (1.0 - lam_init))[None, :]

        (z_lru, dq, dk, dv, gq, gk, gv, nq, nk, nv, gates) = _in_proj(
            xs, modt[l], w_in_k, b_in_k, cd, sd, cg, sg, seg64, qn[l], kn[l], l)
        y_a = _lru(z_lru, lru_conv_w[l], lru_conv_b[l][None, :], wg[l], bg[l], sp[l], n_ctx_tiles)
        y_b = _diff_attn(lam, dq, dk, dv, subln, C)
        y_c = _gqa_attn(gq, gk, gv, C)
        y_d = _na_attn(nq, nk, nv, na_bias, n_ctx_tiles, l)

        skip = n_ctx_tiles if last else 0
        x1 = _merge((y_a, y_b, y_c, y_d), gates, xs, modt[l], wb_k, wo_k, b_out[l][None, :],
                    ln1_g[l][None, :], ln1_b[l][None, :], skip, l)
        xs = _ffn(x1, modt[l], w1_k, ffn_b1[l][None, :], ffn_conv_w[l], ffn_conv_b[l][None, :],
                  w2_k, ffn_b2[l][None, :], ln2_g[l][None, :], ln2_b[l][None, :], n_ctx_tiles, last, l)
    return xs
```

```python
import functools
import math

import numpy as np
import jax
import jax.numpy as jnp
from jax import lax
from jax.experimental import pallas as pl
from jax.experimental.pallas import tpu as pltpu

F32 = jnp.float32
MXU_DTYPE = jnp.bfloat16

DEPTH = 4
GRID_W = 64
LRU_WIDTH = 256
LRU_BLOCKS = 4
LRU_C = 8.0
LRU_CONV = 4
LRU_CONV_LEFT = 2
DIFF_HEADS = 4
DIFF_D = 32
DIFF_V = 64
GQA_Q_HEADS = 4
GQA_KV_HEADS = 2
GQA_D = 64
NA_HEADS = 4
NA_D = 64
NA_WIN_ROWS = 8
NA_WIN_COLS = 16
N_BRANCH = 4
BRANCH_W = 256
FFN_CONV = 3
ROPE_BASE = 10000.0
ALPHA = (2 * DEPTH) ** 0.25
EPS = 1e-6
NEG_BIG = -1e30

TILE = 256
SUBLANES = 8
NA_Q_ROWS = TILE // GRID_W
NA_K_TILES = 3
KEY_CHUNK = 1152
DENOM_ROWS = 16
LOG2E = 1.4426950408889634
FF_CHUNK = 1408
SCAN_ROWS = 8
V7X_VMEM_LIMIT = 56 * 1024 * 1024

_IN_WIDTHS = (256, 256, 256, 256, 256, 256, 128, 128, 256, 256, 256)
_IN_OFFS = tuple(int(v) for v in np.cumsum((0,) + _IN_WIDTHS))


def _params(sem):
    return pltpu.CompilerParams(dimension_semantics=sem, vmem_limit_bytes=V7X_VMEM_LIMIT)


def _const_spec(shape):
    nd = len(shape)
    return pl.BlockSpec(shape, lambda *_: (0,) * nd)


def _layer_spec(shape, l):
    nd = len(shape)
    return pl.BlockSpec((None,) + tuple(shape[1:]), lambda *_: (l,) + (0,) * (nd - 1))


def _ln(x):
    mu = jnp.mean(x, axis=-1, keepdims=True)
    xc = x - mu
    var = jnp.mean(xc * xc, axis=-1, keepdims=True)
    return xc * lax.rsqrt(var + EPS)


def _mm(a, b):
    return jnp.dot(a.astype(MXU_DTYPE), b, preferred_element_type=F32)


def _mm_nt(a, b):
    return lax.dot_general(a, b, (((1,), (1,)), ((), ())), preferred_element_type=F32)


def _seg_mean(x2, seg_ref):
    hi = x2.astype(MXU_DTYPE)
    lo = (x2 - hi.astype(F32)).astype(MXU_DTYPE)
    seg = seg_ref[...]
    return (jnp.dot(hi, seg, preferred_element_type=F32)
            + jnp.dot(lo, seg, preferred_element_type=F32))


def _lane_mask(width, lo, hi):
    lane = lax.broadcasted_iota(jnp.int32, (1, width), 1)
    return (lane >= lo) & (lane < hi)


def _rope(x, cos, sin_signed, dim):
    n = x.shape[-1]
    half = dim // 2
    lane = lax.broadcasted_iota(jnp.int32, (1, n), 1)
    first = (lane & (dim - 1)) < half
    swapped = jnp.where(first, pltpu.roll(x, n - half, 1), pltpu.roll(x, half, 1))
    return x * cos + swapped * sin_signed


def _ada_kernel(c_ref, w_ref, b_ref, o_ref):
    c = c_ref[...]
    s = c * jax.nn.sigmoid(c)
    o_ref[0] = _mm(s, w_ref[0].astype(MXU_DTYPE)) + b_ref[0]


def _ada(cvec, w_ada, b_ada):
    L, D, N = w_ada.shape
    tn = 1536
    R = cvec.shape[0]
    return pl.pallas_call(
        _ada_kernel,
        grid=(L, N // tn),
        in_specs=[pl.BlockSpec((R, D), lambda l, j: (0, 0)),
                  pl.BlockSpec((1, D, tn), lambda l, j: (l, 0, j)),
                  pl.BlockSpec((1, 1, tn), lambda l, j: (l, 0, j))],
        out_specs=pl.BlockSpec((1, R, tn), lambda l, j: (l, 0, j)),
        out_shape=jax.ShapeDtypeStruct((L, R, N), F32),
        compiler_params=_params(("parallel", "parallel")),
        name="ada_mod",
    )(cvec, w_ada, b_ada.reshape(L, 1, N))


def _in_kernel(x_ref, mod_ref, w_ref, b_ref, cd_ref, sd_ref, cg_ref, sg_ref, seg_ref, qn_ref, kn_ref,
               lru_ref, dq_ref, dk_ref, dv_ref, gq_ref, gk_ref, gv_ref, nq_ref, nk_ref, nv_ref,
               gates_ref):
    D = x_ref.shape[-1]
    mod = mod_ref[0]
    h = (_ln(x_ref[0]) * (1.0 + mod[:, D:2 * D]) + mod[:, 0:D]).astype(MXU_DTYPE)

    def proj(lo, hi):
        return jnp.dot(h, w_ref[:, lo:hi], preferred_element_type=F32) + b_ref[:, lo:hi]

    o = _IN_OFFS
    lru_ref[0] = proj(o[0], o[2])

    cd, sd = cd_ref[...], sd_ref[...]
    dq_ref[0] = (_rope(proj(o[2], o[3]), cd, sd, DIFF_D) * (DIFF_D ** -0.5 * LOG2E)).astype(dq_ref.dtype)
    dk_ref[0] = _rope(proj(o[3], o[4]), cd, sd, DIFF_D).astype(dk_ref.dtype)
    dv_ref[0] = proj(o[4], o[5]).T.astype(dv_ref.dtype)

    cg, sg = cg_ref[...], sg_ref[...]
    gq = proj(o[5], o[6])
    gq = gq * lax.rsqrt(_seg_mean(gq * gq, seg_ref) + EPS) * qn_ref[...]
    gq_ref[0] = (_rope(gq, cg, sg, GQA_D) * (GQA_D ** -0.5 * LOG2E)).astype(gq_ref.dtype)
    kw = o[7] - o[6]
    gk = proj(o[6], o[7])
    gk = gk * lax.rsqrt(_seg_mean(gk * gk, seg_ref.at[0:kw, 0:kw]) + EPS) * kn_ref[...]
    gk_ref[0] = _rope(gk, cg[:, 0:kw], sg[:, 0:kw], GQA_D).astype(gk_ref.dtype)
    gv_ref[0] = proj(o[7], o[8]).T.astype(gv_ref.dtype)

    nq_ref[0] = (proj(o[8], o[9]) * (NA_D ** -0.5 * LOG2E)).astype(nq_ref.dtype)
    nk_ref[0] = proj(o[9], o[10]).astype(nk_ref.dtype)
    nv_ref[0] = proj(o[10], o[11]).T.astype(nv_ref.dtype)

    n_gate = gates_ref.shape[-1]
    for j in range(N_BRANCH):
        w = n_gate // N_BRANCH
        gates_ref[0, :, j * w:(j + 1) * w] = jax.nn.sigmoid(
            proj(o[11] + j * w, o[11] + (j + 1) * w)).astype(gates_ref.dtype)


def _in_proj(x, modt, w, b, cd, sd, cg, sg, seg, qn, kn, l):
    B, T, D = x.shape
    nt = T // TILE
    n_gate = w.shape[-1] - _IN_OFFS[11]
    tok = lambda width: pl.BlockSpec((1, TILE, width), lambda bi, i: (bi, i, 0))
    tab = lambda width: pl.BlockSpec((TILE, width), lambda bi, i: (i, 0))
    widths = (512, 256, 256, 256, 256, 128, 128, 256, 256, 256)
    dtypes = (F32,) + (MXU_DTYPE,) * 9
    v_outs = (3, 6, 9)
    tok_t = lambda width: pl.BlockSpec((1, width, TILE), lambda bi, i: (bi, 0, i))
    out_shape = [jax.ShapeDtypeStruct((B, wd, T) if n in v_outs else (B, T, wd), dt)
                 for n, (wd, dt) in enumerate(zip(widths, dtypes))]
    out_shape.append(jax.ShapeDtypeStruct((B, T, n_gate), MXU_DTYPE))
    return pl.pallas_call(
        _in_kernel,
        grid=(B, nt),
        in_specs=[tok(D),
                  pl.BlockSpec((1, 1, 6 * D), lambda bi, i: (2 * bi + jnp.minimum(i, 1), 0, 0)),
                  _layer_spec(w.shape, l), _layer_spec(b.shape, l),
                  tab(256), tab(256), tab(256), tab(256),
                  _const_spec(seg.shape), _const_spec(qn.shape), _const_spec(kn.shape)],
        out_specs=[tok_t(wd) if n in v_outs else tok(wd) for n, wd in enumerate(widths)] + [tok(n_gate)],
        out_shape=out_shape,
        compiler_params=_params(("parallel", "arbitrary")),
        name="in_proj",
    )(x, modt, w, b, cd, sd, cg, sg, seg, qn, kn)


def _gelu_tanh(x):
    return 0.5 * x * (1.0 + jnp.tanh(0.7978845608028654 * (x + 0.044715 * (x * x * x))))


def _lru_kernel(z_ref, cw_ref, cb_ref, wg_ref, bg_ref, sp_ref, y_ref,
                af_ref, bf_ref, ab_ref, bb_ref, hf_ref, hb_ref, *, n_ctx_tiles):
    T = z_ref.shape[1]
    W = LRU_WIDTH
    nt = T // TILE
    last = nt - 1
    a_refs = (af_ref, ab_ref)
    b_refs = (bf_ref, bb_ref)

    def coeff_tile(c, carry):
        r0 = pl.multiple_of(c * TILE, TILE)
        seg_first = (c == 0) | (c == n_ctx_tiles)
        seg_last = (c == n_ctx_tiles - 1) | (c == last)
        p0 = pl.multiple_of(jnp.maximum(r0 - SUBLANES, 0), SUBLANES)
        n0 = pl.multiple_of(jnp.minimum(r0 + TILE, T - SUBLANES), SUBLANES)
        prev = jnp.where(seg_first, 0.0, z_ref[0, pl.ds(p0, SUBLANES), 0:W])
        nxt = jnp.where(seg_last, 0.0, z_ref[0, pl.ds(n0, SUBLANES), 0:W])
        ext = jnp.concatenate([prev, z_ref[0, pl.ds(r0, TILE), 0:W], nxt], axis=0)
        n_ext = TILE + 2 * SUBLANES
        xc = cb_ref[...]
        for k in range(LRU_CONV):
            off = k - LRU_CONV_LEFT
            sh = ext if off == 0 else pltpu.roll(ext, (-off) % n_ext, 0)
            xc = xc + cw_ref[k:k + 1, :] * sh[SUBLANES:SUBLANES + TILE]
        pre = _mm(xc, wg_ref[...]) + bg_ref[...]
        for d in range(2):
            r = jax.nn.sigmoid(pre[:, 2 * d * W:(2 * d + 1) * W])
            gate_i = jax.nn.sigmoid(pre[:, (2 * d + 1) * W:(2 * d + 2) * W])
            log_a = (-LRU_C) * r * sp_ref[d:d + 1, :]
            a2 = jnp.exp(2.0 * log_a)
            mult = jnp.sqrt((1.0 + a2) * jnp.tanh(-log_a))
            a_refs[d][pl.ds(r0, TILE), :] = jnp.exp(log_a)
            b_refs[d][pl.ds(r0, TILE), :] = mult * gate_i * xc
        return carry

    lax.fori_loop(0, nt, coeff_tile, 0)

    row = lax.broadcasted_iota(jnp.int32, (SCAN_ROWS, W), 0)
    n_chunks = T // SCAN_ROWS
    ctx_chunks = n_ctx_tiles * TILE // SCAN_ROWS

    def scan_step(j, carry):
        h_f, h_b = carry
        jf = pl.multiple_of(j * SCAN_ROWS, SCAN_ROWS)
        a = af_ref[pl.ds(jf, SCAN_ROWS), :]
        b = bf_ref[pl.ds(jf, SCAN_ROWS), :]
        for s in (1, 2, 4):
            m = row >= s
            b = jnp.where(m, a * pltpu.roll(b, s, 0) + b, b)
            a = jnp.where(m, a * pltpu.roll(a, s, 0), a)
        h = b + a * h_f
        hf_ref[pl.ds(jf, SCAN_ROWS), :] = h
        h_f = jnp.broadcast_to(h[SCAN_ROWS - 1:SCAN_ROWS, :], (SCAN_ROWS, W))

        cb = jnp.where(j < ctx_chunks, ctx_chunks - 1 - j, n_chunks - 1 + ctx_chunks - j)
        jb = pl.multiple_of(cb * SCAN_ROWS, SCAN_ROWS)
        a = ab_ref[pl.ds(jb, SCAN_ROWS), :]
        b = bb_ref[pl.ds(jb, SCAN_ROWS), :]
        for s in (1, 2, 4):
            m = row < SCAN_ROWS - s
            b = jnp.where(m, a * pltpu.roll(b, SCAN_ROWS - s, 0) + b, b)
            a = jnp.where(m, a * pltpu.roll(a, SCAN_ROWS - s, 0), a)
        h = b + a * h_b
        hb_ref[pl.ds(jb, SCAN_ROWS), :] = h
        h_b = jnp.broadcast_to(h[0:1, :], (SCAN_ROWS, W))
        return h_f, h_b

    zero = jnp.zeros((SCAN_ROWS, W), F32)
    lax.fori_loop(0, n_chunks, scan_step, (zero, zero))

    def out_tile(c, carry):
        r0 = pl.multiple_of(c * TILE, TILE)
        g = z_ref[0, pl.ds(r0, TILE), W:2 * W]
        y_ref[0, pl.ds(r0, TILE), :] = ((hf_ref[pl.ds(r0, TILE), :] + hb_ref[pl.ds(r0, TILE), :])
                                        * _gelu_tanh(g)).astype(y_ref.dtype)
        return carry

    lax.fori_loop(0, nt, out_tile, 0)


def _lru(z, cw, cb, wg, bg, sp, n_ctx_tiles):
    B, T, _ = z.shape
    W = LRU_WIDTH
    return pl.pallas_call(
        functools.partial(_lru_kernel, n_ctx_tiles=n_ctx_tiles),
        grid=(B,),
        in_specs=[pl.BlockSpec((1, T, 2 * W), lambda bi: (bi, 0, 0)),
                  _const_spec(cw.shape), _const_spec(cb.shape), _const_spec(wg.shape),
                  _const_spec(bg.shape), _const_spec(sp.shape)],
        out_specs=pl.BlockSpec((1, T, W), lambda bi: (bi, 0, 0)),
        out_shape=jax.ShapeDtypeStruct((B, T, W), MXU_DTYPE),
        scratch_shapes=[pltpu.VMEM((T, W), F32) for _ in range(6)],
        compiler_params=_params(("parallel",)),
        name="rglru",
    )(z, cw, cb, wg, bg, sp)


def _head_q(q, lo, hi):
    return jnp.where(_lane_mask(q.shape[-1], lo, hi), q.astype(F32), 0.0).astype(MXU_DTYPE)


def _fold_slabs(x, op):
    slabs = [x[r:r + SUBLANES] for r in range(0, x.shape[0], SUBLANES)]
    while len(slabs) > 1:
        nxt = [op(slabs[i], slabs[i + 1]) for i in range(0, len(slabs) - 1, 2)]
        if len(slabs) % 2:
            nxt.append(slabs[-1])
        slabs = nxt
    return slabs[0]


def _attend_maps(maps, s_refs):
    n_maps = len(maps)
    results = []
    m_prev = None
    for n in range(n_maps + 1):
        macc, o_t = None, None
        n_chunks = len(maps[min(n, n_maps - 1)][1])
        off = 0
        for c in range(n_chunks):
            if n < n_maps:
                qm, chunks = maps[n]
                k_fn, bias_fn, _ = chunks[c]
                s = _mm_nt(k_fn(), qm)
                if bias_fn is not None:
                    s = s + bias_fn()
                rows = s.shape[0]
                s_refs[n % 2][off:off + rows, :] = s
                mx = _fold_slabs(s, jnp.maximum)
                macc = mx if macc is None else jnp.maximum(macc, mx)
            if n >= 1:
                _, chunks = maps[n - 1]
                vt = chunks[c][2]()
                rows = vt.shape[1]
                vt = jnp.concatenate([vt, jnp.ones((DENOM_ROWS, rows), vt.dtype)], axis=0)
                e = jnp.exp2(s_refs[(n - 1) % 2][off:off + rows, :] - m_prev)
                o = jnp.dot(vt, e.astype(MXU_DTYPE), preferred_element_type=F32)
                o_t = o if o_t is None else o_t + o
            off += rows
        if n >= 1:
            dv = o_t.shape[0] - DENOM_ROWS
            results.append((o_t[0:dv], o_t[dv:dv + 1]))
        if n < n_maps:
            m_prev = jnp.max(macc, axis=0, keepdims=True)
    return results


def _key_chunks(k_ref, vt_ref, v_rows, nk):
    return [(lambda r0=r0, n=min(KEY_CHUNK, nk - r0): k_ref[0, r0:r0 + n, :], None,
             lambda r0=r0, n=min(KEY_CHUNK, nk - r0): vt_ref[0, v_rows, r0:r0 + n])
            for r0 in range(0, nk, KEY_CHUNK)]


def _diff_kernel(lam_ref, q_ref, k_ref, vt_ref, g_ref, o_ref, s0_ref, s1_ref, *, n_ctx):
    def attend(nk):
        q = q_ref[0]
        lam = lam_ref[...]
        maps = []
        for h in range(DIFF_HEADS):
            chunks = _key_chunks(k_ref, vt_ref, slice(h * DIFF_V, (h + 1) * DIFF_V), nk)
            for mth in range(2):
                c0 = (2 * h + mth) * DIFF_D
                maps.append((_head_q(q, c0, c0 + DIFF_D), chunks))
        res = _attend_maps(maps, (s0_ref, s1_ref))
        heads = []
        for h in range(DIFF_HEADS):
            (o1, l1), (o2, l2) = res[2 * h], res[2 * h + 1]
            o = o1 * (1.0 / l1) - o2 * (lam / l2)
            ms = jnp.mean(o * o, axis=0, keepdims=True)
            heads.append(o * lax.rsqrt(ms + EPS))
        o_ref[0] = (jnp.concatenate(heads, axis=0).T * g_ref[...]).astype(o_ref.dtype)

    i = pl.program_id(1)

    @pl.when(i == 0)
    def _():
        attend(n_ctx)

    @pl.when(i > 0)
    def _():
        attend(k_ref.shape[1])


def _diff_attn(lam, q, k, vt, g, n_ctx):
    B, T, W = q.shape
    return pl.pallas_call(
        functools.partial(_diff_kernel, n_ctx=n_ctx),
        grid=(B, T // TILE),
        in_specs=[_const_spec(lam.shape),
                  pl.BlockSpec((1, TILE, W), lambda bi, i: (bi, i, 0)),
                  pl.BlockSpec((1, T, W), lambda bi, i: (bi, 0, 0)),
                  pl.BlockSpec((1, W, T), lambda bi, i: (bi, 0, 0)),
                  _const_spec(g.shape)],
        out_specs=pl.BlockSpec((1, TILE, W), lambda bi, i: (bi, i, 0)),
        out_shape=jax.ShapeDtypeStruct((B, T, W), MXU_DTYPE),
        scratch_shapes=[pltpu.VMEM((T, TILE), F32), pltpu.VMEM((T, TILE), F32)],
        compiler_params=_params(("parallel", "arbitrary")),
        name="diff_attn",
    )(lam, q, k, vt, g)


def _gqa_kernel(q_ref, k_ref, vt_ref, o_ref, s0_ref, s1_ref, *, n_ctx):
    def attend(nk):
        kvw = k_ref.shape[-1]
        group = GQA_Q_HEADS // GQA_KV_HEADS
        maps = []
        for h in range(GQA_Q_HEADS):
            j, half = h // group, h % group
            q = q_ref[0, :, half * kvw:(half + 1) * kvw]
            chunks = _key_chunks(k_ref, vt_ref, slice(j * GQA_D, (j + 1) * GQA_D), nk)
            maps.append((_head_q(q, j * GQA_D, (j + 1) * GQA_D), chunks))
        res = _attend_maps(maps, (s0_ref, s1_ref))
        o_ref[0] = jnp.concatenate([o * (1.0 / l) for o, l in res], axis=0).T.astype(o_ref.dtype)

    i = pl.program_id(1)

    @pl.when(i == 0)
    def _():
        attend(n_ctx)

    @pl.when(i > 0)
    def _():
        attend(k_ref.shape[1])


def _gqa_attn(q, k, vt, n_ctx):
    B, T, W = q.shape
    KW = k.shape[-1]
    return pl.pallas_call(
        functools.partial(_gqa_kernel, n_ctx=n_ctx),
        grid=(B, T // TILE),
        in_specs=[pl.BlockSpec((1, TILE, W), lambda bi, i: (bi, i, 0)),
                  pl.BlockSpec((1, T, KW), lambda bi, i: (bi, 0, 0)),
                  pl.BlockSpec((1, KW, T), lambda bi, i: (bi, 0, 0))],
        out_specs=pl.BlockSpec((1, TILE, W), lambda bi, i: (bi, i, 0)),
        out_shape=jax.ShapeDtypeStruct((B, T, W), MXU_DTYPE),
        scratch_shapes=[pltpu.VMEM((T, TILE), F32), pltpu.VMEM((T, TILE), F32)],
        compiler_params=_params(("parallel", "arbitrary")),
        name="gqa_attn",
    )(q, k, vt)


def _na_kernel(q_ref, kc_ref, k0_ref, k1_ref, k2_ref, vc_ref, v0_ref, v1_ref, v2_ref, bias_ref, o_ref,
               s0_ref, s1_ref):
    i = pl.program_id(1)

    def run(windowed):
        q = q_ref[0]
        maps = []
        for h in range(NA_HEADS):
            rows = slice(h * NA_D, (h + 1) * NA_D)
            chunks = [(lambda: kc_ref[0], None, lambda rows=rows: vc_ref[0, rows, :])]
            if windowed:
                for t, (kw_ref, vw_ref) in enumerate(((k0_ref, v0_ref), (k1_ref, v1_ref), (k2_ref, v2_ref))):
                    chunks.append((lambda kw_ref=kw_ref: kw_ref[0],
                                   lambda h=h, t=t: bias_ref[0, h, t * TILE:(t + 1) * TILE, :],
                                   lambda rows=rows, vw_ref=vw_ref: vw_ref[0, rows, :]))
            maps.append((_head_q(q, h * NA_D, (h + 1) * NA_D), chunks))
        res = _attend_maps(maps, (s0_ref, s1_ref))
        o_ref[0] = jnp.concatenate([o * (1.0 / l) for o, l in res], axis=0).T.astype(o_ref.dtype)

    @pl.when(i == 0)
    def _():
        run(False)

    @pl.when(i > 0)
    def _():
        run(True)


def _na_attn(q, k, vt, bias, n_ctx_tiles, l):
    B, T, W = q.shape
    nt = T // TILE
    n_lat_tiles = nt - n_ctx_tiles

    def key_tile0(i):
        first = (i - n_ctx_tiles) - (NA_WIN_ROWS // 2) // NA_Q_ROWS
        return n_ctx_tiles + jnp.clip(first, 0, n_lat_tiles - NA_K_TILES)

    def bias_idx(bi, i):
        kind = jnp.where(i <= n_ctx_tiles, 0, jnp.where(i == nt - 1, 2, 1))
        return (l, kind, 0, 0, 0)

    k_spec = lambda t: pl.BlockSpec((1, TILE, W), lambda bi, i: (bi, key_tile0(i) + t, 0))
    v_spec = lambda t: pl.BlockSpec((1, W, TILE), lambda bi, i: (bi, 0, key_tile0(i) + t))
    return pl.pallas_call(
        _na_kernel,
        grid=(B, nt),
        in_specs=[pl.BlockSpec((1, TILE, W), lambda bi, i: (bi, i, 0)),
                  pl.BlockSpec((1, TILE, W), lambda bi, i: (bi, 0, 0)), k_spec(0), k_spec(1), k_spec(2),
                  pl.BlockSpec((1, W, TILE), lambda bi, i: (bi, 0, 0)), v_spec(0), v_spec(1), v_spec(2),
                  pl.BlockSpec((None, 1) + tuple(bias.shape[2:]), bias_idx)],
        out_specs=pl.BlockSpec((1, TILE, W), lambda bi, i: (bi, i, 0)),
        out_shape=jax.ShapeDtypeStruct((B, T, W), MXU_DTYPE),
        scratch_shapes=[pltpu.VMEM(((NA_K_TILES + 1) * TILE, TILE), F32) for _ in range(2)],
        compiler_params=_params(("parallel", "arbitrary")),
        name="na_attn",
    )(q, k, k, k, k, vt, vt, vt, vt, bias)


def _na_bias_table(rpb, n_lat_rows):
    L, H, n_dr, n_dc = rpb.shape
    k_rows = NA_K_TILES * NA_Q_ROWS
    ext = jnp.pad(rpb * LOG2E, ((0, 0), (0, 0), (0, 1), (0, 1)), constant_values=NEG_BIG)
    a = np.arange(NA_Q_ROWS)[:, None]
    j = np.arange(k_rows)[None, :]
    c = np.arange(GRID_W)[:, None]
    c2 = np.arange(GRID_W)[None, :]
    c0 = np.clip(c - NA_WIN_COLS // 2, 0, GRID_W - NA_WIN_COLS)
    col_ok = (c2 >= c0) & (c2 < c0 + NA_WIN_COLS)
    col_sel = np.eye(n_dc + 1, dtype=np.float32)[np.where(col_ok, c2 - c + NA_WIN_COLS - 1, n_dc)]
    half = NA_WIN_ROWS // 2
    last_off = n_lat_rows - NA_WIN_ROWS - (n_lat_rows - k_rows)
    last_rel = (n_lat_rows - NA_Q_ROWS) - (n_lat_rows - k_rows)
    tables = []
    for rel, lo in ((0, 0 * a), (half, a), (last_rel, last_off + 0 * a)):
        row_ok = (j >= lo) & (j < lo + NA_WIN_ROWS)
        row_sel = np.eye(n_dr + 1, dtype=np.float32)[np.where(row_ok, j - rel - a + NA_WIN_ROWS - 1, n_dr)]
        t = jnp.einsum('ajr,lhrs->lhajs', jnp.asarray(row_sel), ext, precision=lax.Precision.HIGHEST)
        t = jnp.einsum('lhajs,cks->lhjkac', t, jnp.asarray(col_sel), precision=lax.Precision.HIGHEST)
        tables.append(t.reshape(L, H, k_rows * GRID_W, NA_Q_ROWS * GRID_W))
    return jnp.stack(tables, axis=1)


def _merge_kernel(ya_ref, yb_ref, yc_ref, yd_ref, gates_ref, x_ref, mod_ref, wb_ref, wo_ref, bo_ref,
                  g_ref, b_ref, o_ref):
    D = x_ref.shape[-1]
    m = jnp.zeros((x_ref.shape[1], D), F32)
    for j, y_ref in enumerate((ya_ref, yb_ref, yc_ref, yd_ref)):
        gate = gates_ref[0, :, j * D:(j + 1) * D].astype(F32)
        m = m + gate * _mm(y_ref[0], wb_ref[j])
    out = _mm(m, wo_ref[...]) + bo_ref[...]
    g1 = mod_ref[0][:, 2 * D:3 * D]
    o_ref[0] = _ln(ALPHA * x_ref[0] + g1 * out) * g_ref[...] + b_ref[...]


def _merge(ys, gates, x, modt, wb, wo, bo, g, b, skip_tiles, l):
    B, T, D = x.shape
    nt = T // TILE - skip_tiles
    tok = lambda width: pl.BlockSpec((1, TILE, width), lambda bi, i: (bi, i + skip_tiles, 0))
    return pl.pallas_call(
        _merge_kernel,
        grid=(B, nt),
        in_specs=[tok(BRANCH_W)] * 4 + [tok(gates.shape[-1]), tok(D),
                  pl.BlockSpec((1, 1, 6 * D), lambda bi, i: (2 * bi + jnp.minimum(i + skip_tiles, 1), 0, 0)),
                  _layer_spec(wb.shape, l), _layer_spec(wo.shape, l), _const_spec(bo.shape),
                  _const_spec(g.shape), _const_spec(b.shape)],
        out_specs=pl.BlockSpec((1, TILE, D), lambda bi, i: (bi, i, 0)),
        out_shape=jax.ShapeDtypeStruct((B, nt * TILE, D), F32),
        compiler_params=_params(("parallel", "arbitrary")),
        name="merge",
    )(*ys, gates, x, modt, wb, wo, bo, g, b)


def _ffn_kernel(xp_ref, x_ref, xn_ref, mod_ref, w1_ref, b1_ref, cw_ref, cb_ref, w2_ref, b2_ref,
                g_ref, b_ref, o_ref, *, seq_first, seq_last):
    D = x_ref.shape[-1]
    d_ff = w2_ref.shape[0]
    i = pl.program_id(1)
    has_prev = functools.reduce(jnp.logical_and, [i != t for t in seq_first])
    has_next = functools.reduce(jnp.logical_and, [i != t for t in seq_last])
    mod = mod_ref[0]
    x = x_ref[0]
    n_ext = TILE + 2 * SUBLANES
    xe = jnp.concatenate([xp_ref[0], x, xn_ref[0]], axis=0)
    h = (_ln(xe) * (1.0 + mod[:, 4 * D:5 * D]) + mod[:, 3 * D:4 * D]).astype(MXU_DTYPE)

    row = lax.broadcasted_iota(jnp.int32, (SUBLANES, 1), 0)
    edge_first = (row == 0) & jnp.logical_not(has_prev)
    edge_last = (row == SUBLANES - 1) & jnp.logical_not(has_next)

    def up(lo, hi):
        u = jnp.dot(h, w1_ref[:, lo:hi], preferred_element_type=F32)
        return jnp.concatenate([jnp.where(has_prev, u[0:SUBLANES], 0.0), u[SUBLANES:SUBLANES + TILE],
                                jnp.where(has_next, u[SUBLANES + TILE:], 0.0)], axis=0)

    def conv(u, lo, hi):
        w0, w1, w2 = cw_ref[0:1, lo:hi], cw_ref[1:2, lo:hi], cw_ref[2:3, lo:hi]
        b1 = b1_ref[:, lo:hi]
        y = (w0 * pltpu.roll(u, 1, 0)[SUBLANES:SUBLANES + TILE]
             + w1 * u[SUBLANES:SUBLANES + TILE]
             + w2 * pltpu.roll(u, n_ext - 1, 0)[SUBLANES:SUBLANES + TILE]) + (cb_ref[:, lo:hi] + b1 * (w0 + w1 + w2))
        return jnp.concatenate([y[0:SUBLANES] - jnp.where(edge_first, b1 * w0, 0.0),
                                y[SUBLANES:TILE - SUBLANES],
                                y[TILE - SUBLANES:TILE] - jnp.where(edge_last, b1 * w2, 0.0)], axis=0)

    chunks = [(lo, min(lo + FF_CHUNK, d_ff)) for lo in range(0, d_ff, FF_CHUNK)]
    acc = jnp.zeros((TILE, D), F32)
    pending = None
    for c in range(len(chunks) + 1):
        nxt = None
        if c < len(chunks):
            lo, hi = chunks[c]
            nxt = (up(lo, hi), up(d_ff + lo, d_ff + hi))
        if pending is not None:
            lo, hi = chunks[c - 1]
            val = conv(pending[0], lo, hi)
            gate = conv(pending[1], d_ff + lo, d_ff + hi)
            acc = acc + _mm(gate * jax.nn.sigmoid(gate) * val, w2_ref[lo:hi, :])
        pending = nxt
    f = acc + b2_ref[...]
    g2 = mod[:, 5 * D:6 * D]
    o_ref[0] = _ln(ALPHA * x + g2 * f) * g_ref[...] + b_ref[...]


def _ffn(x, modt, w1, b1, cw, cb, w2, b2, g, b, n_ctx_tiles, latent_only, l):
    B, T, D = x.shape
    nt = T // TILE
    rows8 = T // SUBLANES
    per = TILE // SUBLANES
    if latent_only:
        seq_first, seq_last = (0,), (nt - 1,)
        mod_idx = lambda bi, i: (2 * bi + 1, 0, 0)
    else:
        seq_first, seq_last = (0, n_ctx_tiles), (n_ctx_tiles - 1, nt - 1)
        mod_idx = lambda bi, i: (2 * bi + jnp.minimum(i, 1), 0, 0)
    return pl.pallas_call(
        functools.partial(_ffn_kernel, seq_first=seq_first, seq_last=seq_last),
        grid=(B, nt),
        in_specs=[pl.BlockSpec((1, SUBLANES, D), lambda bi, i: (bi, jnp.maximum(i * per - 1, 0), 0)),
                  pl.BlockSpec((1, TILE, D), lambda bi, i: (bi, i, 0)),
                  pl.BlockSpec((1, SUBLANES, D), lambda bi, i: (bi, jnp.minimum((i + 1) * per, rows8 - 1), 0)),
                  pl.BlockSpec((1, 1, 6 * D), mod_idx),
                  _layer_spec(w1.shape, l), _const_spec(b1.shape), _const_spec(cw.shape),
                  _const_spec(cb.shape), _layer_spec(w2.shape, l), _const_spec(b2.shape),
                  _const_spec(g.shape), _const_spec(b.shape)],
        out_specs=pl.BlockSpec((1, TILE, D), lambda bi, i: (bi, i, 0)),
        out_shape=jax.ShapeDtypeStruct((B, T, D), F32),
        compiler_params=_params(("parallel", "arbitrary")),
        name="conv_ffn",
    )(x, x, x, modt, w1, b1, cw, cb, w2, b2, g, b)


def _rope_tables(n_lat, n_ctx, dim, width):
    t = jnp.arange(n_lat)
    row = (t // GRID_W).astype(F32)
    col = (t % GRID_W).astype(F32)
    n_freq = dim // 4
    inv = ROPE_BASE ** (-jnp.arange(n_freq, dtype=F32) / n_freq)
    ang = jnp.concatenate([row[:, None] * inv, col[:, None] * inv], -1)
    cos, sin = jnp.cos(ang), jnp.sin(ang)
    cos = jnp.tile(jnp.concatenate([cos, cos], -1), (1, width // dim))
    sin = jnp.tile(jnp.concatenate([-sin, sin], -1), (1, width // dim))
    cos = jnp.concatenate([jnp.ones((n_ctx, width), F32), cos], 0)
    sin = jnp.concatenate([jnp.zeros((n_ctx, width), F32), sin], 0)
    return cos, sin


def _block_diag(w):
    nb, bw, _ = w.shape
    return jnp.einsum('ncd,nm->ncmd', w, jnp.eye(nb, dtype=w.dtype)).reshape(nb * bw, nb * bw)


_GQA_HEAD_ORDER = (0, 2, 1, 3)


def _permute_heads(a, axis, order, d):
    shp = a.shape
    a = a.reshape(shp[:axis] + (len(order), d) + shp[axis + 1:])
    a = jnp.take(a, jnp.asarray(order), axis=axis)
    return a.reshape(shp)


def kernel(x, c, ctx, c_ctx, w_ada, b_ada, w_in, b_in, lru_conv_w, lru_conv_b, lru_w_a, lru_b_a, lru_w_x, lru_b_x, lru_lambda, diff_lambda, diff_subln, gqa_q_norm, gqa_k_norm, na_rpb, w_branch, w_out, b_out, ln1_g, ln1_b, ffn_w1, ffn_b1, ffn_conv_w, ffn_conv_b, ffn_w2, ffn_b2, ln2_g, ln2_b):
    B, S, D = x.shape
    C = ctx.shape[1]
    L = w_ada.shape[0]
    assert C == TILE and S % TILE == 0 and S // TILE >= NA_K_TILES
    n_ctx_tiles = C // TILE

    pad = (-(B + 1)) % SUBLANES
    cvec = jnp.concatenate([c, c_ctx[None], jnp.zeros((pad, D), F32)], 0)
    mod_all = _ada(cvec, w_ada, b_ada)
    mod_lat = mod_all[:, :B]
    mod_ctx = jnp.broadcast_to(mod_all[:, B:B + 1], mod_lat.shape)
    modt = jnp.stack([mod_ctx, mod_lat], axis=2).reshape(L, 2 * B, 1, 6 * D)

    cd, sd = _rope_tables(S, C, DIFF_D, 256)
    cg, sg = _rope_tables(S, C, GQA_D, 256)
    seg64 = jnp.asarray(np.kron(np.eye(4), np.full((64, 64), 1.0 / 64)), MXU_DTYPE)

    o = _IN_OFFS
    gq_cols = _permute_heads(w_in[:, :, o[5]:o[6]], 2, _GQA_HEAD_ORDER, GQA_D)
    w_in_k = jnp.concatenate([w_in[:, :, :o[5]], gq_cols, w_in[:, :, o[6]:]], 2).astype(MXU_DTYPE)
    gq_b = _permute_heads(b_in[:, o[5]:o[6]], 1, _GQA_HEAD_ORDER, GQA_D)
    b_in_k = jnp.concatenate([b_in[:, :o[5]], gq_b, b_in[:, o[6]:]], 1)[:, None, :]
    wb_k = w_branch.astype(MXU_DTYPE)
    wo_k = w_out.astype(MXU_DTYPE)
    w1_k = ffn_w1.astype(MXU_DTYPE)
    w2_k = ffn_w2.astype(MXU_DTYPE)
    qn = jnp.tile(gqa_q_norm, (1, GQA_Q_HEADS))[:, None, :]
    kn = jnp.tile(gqa_k_norm, (1, GQA_KV_HEADS))[:, None, :]
    wg = jnp.concatenate([jax.vmap(_block_diag)(lru_w_a[:, 0]), jax.vmap(_block_diag)(lru_w_x[:, 0]),
                          jax.vmap(_block_diag)(lru_w_a[:, 1]), jax.vmap(_block_diag)(lru_w_x[:, 1])],
                         -1).astype(MXU_DTYPE)
    bg = jnp.concatenate([lru_b_a[:, 0].reshape(L, -1), lru_b_x[:, 0].reshape(L, -1),
                          lru_b_a[:, 1].reshape(L, -1), lru_b_x[:, 1].reshape(L, -1)], -1)[:, None, :]
    sp = jax.nn.softplus(-lru_lambda.astype(F32))
    lp = diff_lambda.astype(F32)
    lam_delta = jnp.exp(jnp.sum(lp[:, 0] * lp[:, 1], -1)) - jnp.exp(jnp.sum(lp[:, 2] * lp[:, 3], -1))
    na_bias = _na_bias_table(na_rpb.astype(F32), S // GRID_W)

    xs = jnp.concatenate([ctx, x], axis=1)
    for l in range(L):
        last = l == L - 1
        lam_init = 0.8 - 0.6 * math.exp(-0.3 * l)
        lam = (lam_delta[l] + lam_init).reshape(1, 1)
        subln = (jnp.tile(diff_subln[l], DIFF_HEADS) * (1.0 - lam_init))[None, :]

        (z_lru, dq, dk, dv, gq, gk, gv, nq, nk, nv, gates) = _in_proj(
            xs, modt[l], w_in_k, b_in_k, cd, sd, cg, sg, seg64, qn[l], kn[l], l)
        y_a = _lru(z_lru, lru_conv_w[l], lru_conv_b[l][None, :], wg[l], bg[l], sp[l], n_ctx_tiles)
        y_b = _diff_attn(lam, dq, dk, dv, subln, C)
        y_c = _gqa_attn(gq, gk, gv, C)
        y_d = _na_attn(nq, nk, nv, na_bias, n_ctx_tiles, l)

        skip = n_ctx_tiles if last else 0
        x1 = _merge((y_a, y_b, y_c, y_d), gates, xs, modt[l], wb_k, wo_k, b_out[l][None, :],
                    ln1_g[l][None, :], ln1_b[l][None, :], skip, l)
        xs = _ffn(x1, modt[l], w1_k, ffn_b1[l][None, :], ffn_conv_w[l], ffn_conv_b[l][None, :],
                  w2_k, ffn_b2[l][None, :], ln2_g[l][None, :], ln2_b[l][None, :], n_ctx_tiles, last, l)
    return xs
```
